```python
import math
import jax
import jax.numpy as jnp
from jax import lax
import numpy as np

D_MODEL = 1024
BATCH = 8
SEQ = 8192
DEPTH = 2

GROUP_WIDTH = D_MODEL // 2
D_MIX = 3 * GROUP_WIDTH
ATTN_HEADS = 4
ATTN_HEAD_DIM = GROUP_WIDTH // (2 * ATTN_HEADS)
Q_BLOCK = 128
SSD_HEAD_DIM = 64
SSD_HEADS = GROUP_WIDTH // SSD_HEAD_DIM
SSD_GROUPS = 2
SSD_STATE = 128
SSD_CONV = 4
SSD_CHUNK = 128
SSD_XBC = GROUP_WIDTH + 2 * SSD_GROUPS * SSD_STATE
SC_WIDTH = GROUP_WIDTH
SC_CONV = 3
D_FF = 2816
N_EXPERTS = 8
TOP_K = 2
D_FF_EXPERT = 3584
MOE_BLOCK = 256
EPS = 1e-6

QK_WIDTH = ATTN_HEADS * 2 * ATTN_HEAD_DIM
IN_WIDTHS = (QK_WIDTH, QK_WIDTH, ATTN_HEADS * 2 * ATTN_HEAD_DIM,
             GROUP_WIDTH, SSD_XBC, SSD_HEADS,
             SC_WIDTH, SC_WIDTH, SC_WIDTH)
IN_COLS = sum(IN_WIDTHS)
IN_SPLITS = tuple(int(v) for v in np.cumsum(IN_WIDTHS)[:-1])

kernel_name = 'hymba_style_diffattn_ssd_shortconv_moe'


def rms_norm(x, g):
    xf = x.astype(jnp.float32)
    y = xf * lax.rsqrt(jnp.mean(xf * xf, axis=-1, keepdims=True) + EPS)
    return (y * g.astype(jnp.float32)).astype(x.dtype)


def gated_group_rmsnorm(y, z, g, groups):
    v = y.astype(jnp.float32) * jax.nn.silu(z.astype(jnp.float32))
    shp = v.shape
    v = v.reshape(shp[:-1] + (groups, shp[-1] // groups))
    v = v * lax.rsqrt(jnp.mean(v * v, axis=-1, keepdims=True) + EPS)
    return v.reshape(shp) * g.astype(jnp.float32)


def causal_depthwise_conv(u, w):
    k, ch = w.shape
    return lax.conv_general_dilated(
        u, w[:, None, :].astype(u.dtype), window_strides=(1,), padding=((k - 1, 0),),
        dimension_numbers=('NWC', 'WIO', 'NWC'), feature_group_count=ch)


def diff_attention(q, k, v, lam):
    b, s, h, _, d = q.shape
    nq = s // Q_BLOCK
    qb = jnp.moveaxis(q.reshape(b, nq, Q_BLOCK, h, 2, d), 1, 0)
    key_pos = jnp.arange(s)
    scale = d ** -0.5

    def block(args):
        q_blk, blk_idx = args
        scores = jnp.einsum('bqhmd,bkhmd->bhmqk', q_blk, k).astype(jnp.float32) * scale
        q_pos = blk_idx * Q_BLOCK + jnp.arange(Q_BLOCK)
        causal = key_pos[None, :] <= q_pos[:, None]
        p = jax.nn.softmax(jnp.where(causal, scores, -jnp.inf), axis=-1)
        p_diff = p[:, :, 0] - lam * p[:, :, 1]
        return jnp.einsum('bhqk,bkhe->bqhe', p_diff.astype(v.dtype), v)

    out = lax.map(block, (qb, jnp.arange(nq)))
    return jnp.moveaxis(out, 0, 1).reshape(b, s, h, 2 * d)


def ssd_chunked_scan(xh, dt, a, bm, cm):
    b, s, h, p = xh.shape
    g, n = bm.shape[2], bm.shape[3]
    r = h // g
    L = SSD_CHUNK
    nc = s // L
    x = xh.reshape(b, nc, L, g, r, p)
    dt = dt.reshape(b, nc, L, g, r)
    bm = bm.reshape(b, nc, L, g, n)
    cm = cm.reshape(b, nc, L, g, n)
    a_cum = jnp.cumsum(dt * a.reshape(g, r), axis=2)
    xdt = x * dt[..., None]
    causal = jnp.tril(jnp.ones((L, L), dtype=bool))[:, :, None, None]
    seg = a_cum[:, :, :, None] - a_cum[:, :, None, :]
    decay = jnp.exp(jnp.where(causal, seg, -jnp.inf))
    cb = jnp.einsum('bclgn,bcsgn->bclsg', cm, bm)
    y_diag = jnp.einsum('bclsgr,bcsgrp->bclgrp', decay * cb[..., None], xdt)
    decay_to_end = jnp.exp(a_cum[:, :, -1:] - a_cum)
    states = jnp.einsum('bclgn,bclgrp->bcgrpn', bm, xdt * decay_to_end[..., None])
    chunk_decay = jnp.exp(a_cum[:, :, -1])

    def step(carry, inp):
        st, dec = inp
        return carry * dec[..., None, None] + st, carry

    init = jnp.zeros((b, g, r, p, n), xh.dtype)
    _, prev = lax.scan(step, init, (jnp.moveaxis(states, 1, 0), jnp.moveaxis(chunk_decay, 1, 0)))
    prev = jnp.moveaxis(prev, 0, 1)
    y_off = jnp.einsum('bclgn,bcgrpn->bclgrp', cm, prev) * jnp.exp(a_cum)[..., None]
    return (y_diag + y_off).reshape(b, s, h, p)


def hybrid_mixer(h, layer_idx, w_in, w_out, lambda_qk, attn_subln, conv_w, conv_b,
                 dt_bias, a_log, d_skip, ssd_norm_g, sconv_w):
    b, s, _ = h.shape
    f32 = jnp.float32
    q, k, v, z, xbc, dt, gate_b, gate_c, u = jnp.split(h @ w_in, IN_SPLITS, axis=-1)
    q = q.reshape(b, s, ATTN_HEADS, 2, ATTN_HEAD_DIM)
    k = k.reshape(b, s, ATTN_HEADS, 2, ATTN_HEAD_DIM)
    v = v.reshape(b, s, ATTN_HEADS, 2 * ATTN_HEAD_DIM)
    lam_init = 0.8 - 0.6 * math.exp(-0.3 * layer_idx)
    lq = lambda_qk.astype(f32)
    lam = jnp.exp(jnp.sum(lq[0] * lq[1])) - jnp.exp(jnp.sum(lq[2] * lq[3])) + lam_init
    attn = diff_attention(q, k, v, lam)
    attn = (rms_norm(attn, attn_subln) * (1.0 - lam_init)).reshape(b, s, GROUP_WIDTH)
    xbc = jax.nn.silu(causal_depthwise_conv(xbc, conv_w) + conv_b)
    xs, bm, cm = jnp.split(xbc, (GROUP_WIDTH, GROUP_WIDTH + SSD_GROUPS * SSD_STATE), axis=-1)
    dt = jax.nn.softplus(dt.astype(f32) + dt_bias.astype(f32))
    a = -jnp.exp(a_log.astype(f32))
    xh = xs.reshape(b, s, SSD_HEADS, SSD_HEAD_DIM).astype(f32)
    ys = ssd_chunked_scan(xh, dt, a,
                          bm.reshape(b, s, SSD_GROUPS, SSD_STATE).astype(f32),
                          cm.reshape(b, s, SSD_GROUPS, SSD_STATE).astype(f32))
    ys = (ys + d_skip.astype(f32)[:, None] * xh).reshape(b, s, GROUP_WIDTH)
    ssd = gated_group_rmsnorm(ys, z, ssd_norm_g, SSD_GROUPS).astype(h.dtype)
    sconv = gate_b * causal_depthwise_conv(gate_c * u, sconv_w)
    return jnp.concatenate([attn, ssd, sconv], axis=-1) @ w_out


def swiglu(h, w1, w3, w2):
    return (jax.nn.silu(h @ w1) * (h @ w3)) @ w2


def moe_swiglu(h, w_router, w1, w3, w2):
    b, s, d = h.shape
    xt = h.reshape(-1, d)
    n = xt.shape[0]
    logits = (xt @ w_router).astype(jnp.float32)
    top_logits, top_idx = lax.top_k(logits, TOP_K)
    gates = jax.nn.softmax(top_logits, axis=-1)
    n_assign = n * TOP_K
    e_flat = top_idx.reshape(-1)
    g_flat = gates.reshape(-1)
    t_flat = jnp.repeat(jnp.arange(n, dtype=jnp.int32), TOP_K)
    order = jnp.argsort(e_flat)
    e_sorted, t_sorted, g_sorted = e_flat[order], t_flat[order], g_flat[order]
    counts = jnp.bincount(e_flat, length=N_EXPERTS)
    padded = (counts + MOE_BLOCK - 1) // MOE_BLOCK * MOE_BLOCK
    start = jnp.cumsum(counts) - counts
    pstart = jnp.cumsum(padded) - padded
    dest = pstart[e_sorted] + jnp.arange(n_assign) - start[e_sorted]
    p_len = -(-n_assign // MOE_BLOCK) * MOE_BLOCK + N_EXPERTS * MOE_BLOCK
    n_blocks = p_len // MOE_BLOCK
    tok_buf = jnp.zeros((p_len,), jnp.int32).at[dest].set(t_sorted)
    gate_buf = jnp.zeros((p_len,), jnp.float32).at[dest].set(g_sorted)
    block_expert = jnp.minimum(
        jnp.searchsorted(jnp.cumsum(padded), jnp.arange(n_blocks) * MOE_BLOCK, side='right'),
        N_EXPERTS - 1)
    xb = xt[tok_buf].reshape(n_blocks, MOE_BLOCK, d)

    def expert_block(args):
        x_blk, e = args
        return swiglu(x_blk, w1[e], w3[e], w2[e])

    yb = lax.map(expert_block, (xb, block_expert)).reshape(p_len, d)
    y = jax.ops.segment_sum(yb * gate_buf[:, None].astype(yb.dtype), tok_buf, num_segments=n)
    return y.reshape(b, s, d)


def setup_inputs(seed: int = 0) -> dict:
    key = jax.random.key(seed)
    ks = jax.random.split(key, 26)
    f32 = jnp.float32
    n_dense = (DEPTH + 1) // 2
    n_moe = DEPTH // 2

    def nrm(i, shape, scale):
        return jax.random.normal(ks[i], shape, f32) * scale

    def gain(i, shape):
        return 1.0 + 0.05 * jax.random.normal(ks[i], shape, f32)

    dt0 = jnp.exp(jax.random.uniform(ks[14], (DEPTH, SSD_HEADS), f32, math.log(1e-3), math.log(1e-1)))
    return {
        'x': nrm(0, (BATCH, SEQ, D_MODEL), 1.0),
        'c': nrm(1, (BATCH, D_MODEL), 1.0),
        'w_mod': nrm(2, (DEPTH, D_MODEL, 6 * D_MODEL), D_MODEL ** -0.5),
        'b_mod': nrm(3, (DEPTH, 6 * D_MODEL), 0.02),
        'norm_mix_pre': gain(4, (DEPTH, D_MODEL)),
        'norm_mix_post': gain(5, (DEPTH, D_MODEL)),
        'norm_ffn_pre': gain(6, (DEPTH, D_MODEL)),
        'norm_ffn_post': gain(7, (DEPTH, D_MODEL)),
        'w_in': nrm(8, (DEPTH, D_MODEL, IN_COLS), D_MODEL ** -0.5),
        'w_out': nrm(9, (DEPTH, D_MIX, D_MODEL), D_MIX ** -0.5),
        'lambda_qk': nrm(10, (DEPTH, 4, ATTN_HEAD_DIM), 0.1),
        'attn_subln': gain(11, (DEPTH, 2 * ATTN_HEAD_DIM)),
        'ssd_conv_w': nrm(12, (DEPTH, SSD_CONV, SSD_XBC), SSD_CONV ** -0.5),
        'ssd_conv_b': nrm(13, (DEPTH, SSD_XBC), 0.02),
        'ssd_dt_bias': dt0 + jnp.log(-jnp.expm1(-dt0)),
        'ssd_a_log': jnp.log(jax.random.uniform(ks[15], (DEPTH, SSD_HEADS), f32, 1.0, 16.0)),
        'ssd_d': gain(16, (DEPTH, SSD_HEADS)),
        'ssd_norm': gain(17, (DEPTH, GROUP_WIDTH)),
        'sconv_w': nrm(18, (DEPTH, SC_CONV, SC_WIDTH), SC_CONV ** -0.5),
        'ffn_w1': nrm(19, (n_dense, D_MODEL, D_FF), D_MODEL ** -0.5),
        'ffn_w3': nrm(20, (n_dense, D_MODEL, D_FF), D_MODEL ** -0.5),
        'ffn_w2': nrm(21, (n_dense, D_FF, D_MODEL), D_FF ** -0.5),
        'moe_router': nrm(22, (n_moe, D_MODEL, N_EXPERTS), D_MODEL ** -0.5),
        'moe_w1': nrm(23, (n_moe, N_EXPERTS, D_MODEL, D_FF_EXPERT), D_MODEL ** -0.5),
        'moe_w3': nrm(24, (n_moe, N_EXPERTS, D_MODEL, D_FF_EXPERT), D_MODEL ** -0.5),
        'moe_w2': nrm(25, (n_moe, N_EXPERTS, D_FF_EXPERT, D_MODEL), D_FF_EXPERT ** -0.5),
    }


def reference(x, c, w_mod, b_mod, norm_mix_pre, norm_mix_post, norm_ffn_pre, norm_ffn_post,
              w_in, w_out, lambda_qk, attn_subln, ssd_conv_w, ssd_conv_b, ssd_dt_bias,
              ssd_a_log, ssd_d, ssd_norm, sconv_w, ffn_w1, ffn_w3, ffn_w2,
              moe_router, moe_w1, moe_w3, moe_w2):
    c_act = jax.nn.silu(c)
    for i in range(DEPTH):
        mod = c_act @ w_mod[i] + b_mod[i]
        shift_m, scale_m, gate_m, shift_f, scale_f, gate_f = [m[:, None, :] for m in jnp.split(mod, 6, axis=-1)]
        h = rms_norm(x, norm_mix_pre[i]) * (1.0 + scale_m) + shift_m
        y = hybrid_mixer(h, i, w_in[i], w_out[i], lambda_qk[i], attn_subln[i],
                         ssd_conv_w[i], ssd_conv_b[i], ssd_dt_bias[i], ssd_a_log[i],
                         ssd_d[i], ssd_norm[i], sconv_w[i])
        x = x + gate_m * rms_norm(y, norm_mix_post[i])
        h = rms_norm(x, norm_ffn_pre[i]) * (1.0 + scale_f) + shift_f
        if i % 2 == 0:
            y = swiglu(h, ffn_w1[i // 2], ffn_w3[i // 2], ffn_w2[i // 2])
        else:
            y = moe_swiglu(h, moe_router[i // 2], moe_w1[i // 2], moe_w3[i // 2], moe_w2[i // 2])
        x = x + gate_f * rms_norm(y, norm_ffn_post[i])
    return x
```

```python
import functools
import math

import jax
import jax.numpy as jnp
from jax import lax
from jax.experimental import pallas as pl
from jax.experimental.pallas import tpu as pltpu

F32 = jnp.float32
BF16 = jnp.bfloat16
I32 = jnp.int32

EPS = 1e-6
GROUP_WIDTH = 512
ATTN_HEADS = 4
ATTN_HEAD_DIM = 64
SSD_HEAD_DIM = 64
SSD_HEADS = 8
SSD_GROUPS = 2
SSD_STATE = 128
SSD_CONV = 4
SSD_CHUNK = 128
SSD_XBC = GROUP_WIDTH + 2 * SSD_GROUPS * SSD_STATE
SC_CONV = 3
N_EXPERTS = 8
TOP_K = 2

LANES = 128
SUBLANES = 8
BF16_ROWS = 16
VMEM_LIMIT = 48 * 1024 * 1024
NEG_BIG = -1e30

COL_Q, COL_K, COL_V, COL_Z, COL_XBC, COL_GB, COL_GC, COL_U = 0, 512, 1024, 1536, 2048, 3072, 3584, 4096
PROJ_COLS = 4608

NT_DIMS = (((1,), (1,)), ((), ()))


def _dot(a, b):
    return jnp.dot(a, b, preferred_element_type=F32)


def _dot_nt(a, b):
    return lax.dot_general(a, b, NT_DIMS, preferred_element_type=F32)


def _split2(x):
    hi = x.astype(BF16)
    lo = (x - hi.astype(F32)).astype(BF16)
    return hi, lo


def _split3(x):
    hi = x.astype(BF16)
    r = x - hi.astype(F32)
    mid = r.astype(BF16)
    lo = (r - mid.astype(F32)).astype(BF16)
    return hi, mid, lo


def _sigmoid(x):
    return 1.0 / (1.0 + jnp.exp(-x))


def _silu(x):
    return x * _sigmoid(x)


def _softplus(x):
    return jnp.maximum(x, 0.0) + jnp.log(1.0 + jnp.exp(-jnp.abs(x)))


def _rms(x, g):
    return x * lax.rsqrt(jnp.mean(x * x, axis=-1, keepdims=True) + EPS) * g


def _params(sem):
    return pltpu.CompilerParams(dimension_semantics=sem, vmem_limit_bytes=VMEM_LIMIT)


def _pick(n, pref):
    t = min(n, pref)
    assert n % t == 0, (n, pref)
    return t


def _mod_kernel(c_ref, w_ref, b_ref, o_ref):
    c = c_ref[...]
    ah, al = _split2(_silu(c))
    wh, wl = _split2(w_ref[...])
    o_ref[...] = _dot(ah, wh) + _dot(ah, wl) + _dot(al, wh) + b_ref[...]


def _modulation(c, w_mod, b_mod):
    depth, d, n6 = w_mod.shape
    b = c.shape[0]
    tn = _pick(n6, 1536)
    return pl.pallas_call(
        _mod_kernel,
        out_shape=jax.ShapeDtypeStruct((depth, b, n6), F32),
        grid=(depth, n6 // tn),
        in_specs=[
            pl.BlockSpec((b, d), lambda l, j: (0, 0)),
            pl.BlockSpec((None, d, tn), lambda l, j: (l, 0, j)),
            pl.BlockSpec((None, 1, tn), lambda l, j: (l, 0, j)),
        ],
        out_specs=pl.BlockSpec((None, b, tn), lambda l, j: (l, 0, j)),
        compiler_params=_params(("parallel", "parallel")),
        name="modulation",
    )(c, w_mod, b_mod.reshape(depth, 1, n6))


def _inproj_kernel(x_ref, mod_ref, g_ref, w_ref, wdt_ref, o_ref, dt_ref, h_ref):
    @pl.when(pl.program_id(1) == 0)
    def _():
        h = _rms(x_ref[...], g_ref[...]) * (1.0 + mod_ref[1:2, :]) + mod_ref[0:1, :]
        hb = h.astype(BF16)
        h_ref[...] = hb
        dt_ref[...] = _dot(hb, wdt_ref[...])

    o_ref[...] = _dot(h_ref[...], w_ref[...]).astype(BF16)


def _inproj(x2, mod_l, g, w, wdt, rows_per_batch):
    n, d = x2.shape
    tm = _pick(rows_per_batch, 1024)
    tn = 1536
    bpb = rows_per_batch // tm
    return pl.pallas_call(
        _inproj_kernel,
        out_shape=(jax.ShapeDtypeStruct((n, PROJ_COLS), BF16),
                   jax.ShapeDtypeStruct((n, LANES), F32)),
        grid=(n // tm, PROJ_COLS // tn),
        in_specs=[
            pl.BlockSpec((tm, d), lambda i, j: (i, 0)),
            pl.BlockSpec((None, 6, d), lambda i, j: (i // bpb, 0, 0)),
            pl.BlockSpec((1, d), lambda i, j: (0, 0)),
            pl.BlockSpec((d, tn), lambda i, j: (0, j)),
            pl.BlockSpec((d, LANES), lambda i, j: (0, 0)),
        ],
        out_specs=(pl.BlockSpec((tm, tn), lambda i, j: (i, j)),
                   pl.BlockSpec((tm, LANES), lambda i, j: (i, 0))),
        scratch_shapes=[pltpu.VMEM((tm, d), BF16)],
        compiler_params=_params(("parallel", "arbitrary")),
        name="inproj",
    )(x2, mod_l, g, w, wdt)


def _attn_kernel(q_ref, k_ref, v_ref, lq_ref, sub_ref, o_ref, qs_ref, m_ref, l_ref, acc_ref,
                 *, tq, lam_init):
    i = pl.program_id(2)
    d = ATTN_HEAD_DIM
    q = q_ref[...].astype(F32) * (d ** -0.5)
    lane = lax.broadcasted_iota(I32, q.shape, 1)
    qs_ref[0:tq, :] = jnp.where(lane < d, q, 0.0).astype(BF16)
    qs_ref[tq:, :] = jnp.where(lane >= d, q, 0.0).astype(BF16)
    m_ref[...] = jnp.full(m_ref.shape, NEG_BIG, F32)
    l_ref[...] = jnp.zeros(l_ref.shape, F32)
    acc_ref[...] = jnp.zeros(acc_ref.shape, F32)

    def chunk(j, masked):
        start = pl.multiple_of(j * tq, tq)
        k = k_ref[pl.ds(start, tq), :]
        v = v_ref[pl.ds(start, tq), :]
        s = _dot_nt(qs_ref[...], k)
        if masked:
            r = lax.broadcasted_iota(I32, (tq, tq), 0)
            c = lax.broadcasted_iota(I32, (tq, tq), 1)
            keep = c <= r
            s = jnp.where(jnp.concatenate([keep, keep], axis=0), s, NEG_BIG)
        m_prev = m_ref[...]
        m_new = jnp.maximum(m_prev, jnp.max(s, axis=-1, keepdims=True))
        alpha = jnp.exp(m_prev - m_new)
        p = jnp.exp(s - m_new)
        l_ref[...] = alpha * l_ref[...] + jnp.sum(p, axis=-1, keepdims=True)
        acc_ref[...] = alpha * acc_ref[...] + _dot(p.astype(BF16), v)
        m_ref[...] = m_new

    def body(j, carry):
        chunk(j, False)
        return carry

    lax.fori_loop(0, i, body, 0)
    chunk(i, True)

    lq = lq_ref[...]
    lam = (jnp.exp(jnp.sum(lq[0:1, :] * lq[1:2, :], axis=-1, keepdims=True))
           - jnp.exp(jnp.sum(lq[2:3, :] * lq[3:4, :], axis=-1, keepdims=True)) + lam_init)
    o1 = acc_ref[0:tq, :] / l_ref[0:tq, :]
    o2 = acc_ref[tq:, :] / l_ref[tq:, :]
    o = o1 - lam * o2
    o_ref[...] = (_rms(o, sub_ref[...]) * (1.0 - lam_init)).astype(BF16)


def _attention(proj, lambda_qk, subln, batch, seq, lam_init):
    n = proj.shape[0]
    tq = _pick(seq, 512)
    nq = seq // tq
    h = ATTN_HEADS
    return pl.pallas_call(
        functools.partial(_attn_kernel, tq=tq, lam_init=lam_init),
        out_shape=jax.ShapeDtypeStruct((n, GROUP_WIDTH), BF16),
        grid=(batch, h, nq),
        in_specs=[
            pl.BlockSpec((tq, LANES), lambda b, hh, i: (b * nq + i, COL_Q // LANES + hh)),
            pl.BlockSpec((seq, LANES), lambda b, hh, i: (b, COL_K // LANES + hh)),
            pl.BlockSpec((seq, LANES), lambda b, hh, i: (b, COL_V // LANES + hh)),
            pl.BlockSpec((4, ATTN_HEAD_DIM), lambda b, hh, i: (0, 0)),
            pl.BlockSpec((1, LANES), lambda b, hh, i: (0, 0)),
        ],
        out_specs=pl.BlockSpec((tq, LANES), lambda b, hh, i: (b * nq + i, hh)),
        scratch_shapes=[
            pltpu.VMEM((2 * tq, LANES), BF16),
            pltpu.VMEM((2 * tq, 1), F32),
            pltpu.VMEM((2 * tq, 1), F32),
            pltpu.VMEM((2 * tq, LANES), F32),
        ],
        compiler_params=_params(("parallel", "parallel", "arbitrary")),
        name="diff_attention",
    )(proj, proj, proj, lambda_qk, subln)


def _causal_conv(x, carry, w, width):
    row = lax.broadcasted_iota(I32, carry.shape, 0)
    out = x * w[width - 1:width, :]
    for k in range(1, width):
        xr = pltpu.roll(x, k, 0)
        cr = pltpu.roll(carry, k, 0)
        head = jnp.where(row < k, cr, xr[0:SUBLANES, :])
        xk = jnp.concatenate([head, xr[SUBLANES:, :]], axis=0)
        out = out + xk * w[width - 1 - k:width - k, :]
    return out


def _ssd_kernel(z_ref, xbc_ref, gb_ref, gc_ref, u_ref, dt_ref, cw_ref, cb_ref, dtb_ref, alog_ref,
                dskip_ref, ng_ref, sw_ref, o_ref,
                cx_ref, cs_ref, state_ref, xs_ref, bm_ref, cm_ref, *, tc):
    L = SSD_CHUNK
    gw = GROUP_WIDTH
    ns = SSD_STATE

    @pl.when(pl.program_id(1) == 0)
    def _():
        cx_ref[...] = jnp.zeros(cx_ref.shape, F32)
        cs_ref[...] = jnp.zeros(cs_ref.shape, F32)
        state_ref[...] = jnp.zeros(state_ref.shape, F32)

    pu = gc_ref[...].astype(F32) * u_ref[...].astype(F32)
    sconv = gb_ref[...].astype(F32) * _causal_conv(pu, cs_ref[...], sw_ref[...], SC_CONV)
    cs_ref[...] = pu[tc - SUBLANES:, :]
    o_ref[:, gw:] = sconv.astype(BF16)

    xbc = xbc_ref[...].astype(F32)
    act = _silu(_causal_conv(xbc, cx_ref[...], cw_ref[...], SSD_CONV) + cb_ref[...])
    cx_ref[...] = xbc[tc - SUBLANES:, :]
    xs_ref[...] = act[:, 0:gw]
    bm_ref[...] = act[:, gw:gw + SSD_GROUPS * ns]
    cm_ref[...] = act[:, gw + SSD_GROUPS * ns:]

    lane = lax.broadcasted_iota(I32, (L, LANES), 1)
    lo = lane < SSD_HEAD_DIM
    rr = lax.broadcasted_iota(I32, (L, L), 0)
    cc = lax.broadcasted_iota(I32, (L, L), 1)
    tril = cc <= rr
    ltri = jnp.where(tril, 1.0, 0.0).astype(BF16)
    head_lane = lane < SSD_HEADS
    a_row = -jnp.exp(alog_ref[...])

    def pair_pattern(mat, h0):
        lo_b = lo[0:mat.shape[0], :]
        return jnp.where(lo_b, mat[:, h0:h0 + 1], mat[:, h0 + 1:h0 + 2])

    def chunk_body(c, carry):
        r0 = pl.multiple_of(c * L, L)
        rows = pl.ds(r0, L)
        dt = _softplus(dt_ref[rows, :] + dtb_ref[...])
        da = jnp.where(head_lane, dt * a_row, 0.0)
        d_hi, d_mid, d_lo = _split3(da)
        acum = _dot(ltri, d_hi) + _dot(ltri, d_mid) + _dot(ltri, d_lo)
        acum_t = acum.T
        a_last = acum[L - 1:L, :]
        ys = []
        for g in range(SSD_GROUPS):
            bg = bm_ref[rows, g * ns:(g + 1) * ns]
            cg = cm_ref[rows, g * ns:(g + 1) * ns].astype(BF16)
            cb = _dot_nt(cg, bg.astype(BF16))
            bg_t = bg.T.astype(BF16)
            for jp in range(SSD_HEADS // SSD_GROUPS // 2):
                j = g * (SSD_HEADS // SSD_GROUPS // 2) + jp
                h0 = 2 * j
                xs = xs_ref[rows, j * LANES:(j + 1) * LANES]
                xdt = xs * pair_pattern(dt, h0)
                y = None
                for hh, keep in ((h0, lo), (h0 + 1, jnp.logical_not(lo))):
                    seg = acum[:, hh:hh + 1] - acum_t[hh:hh + 1, :]
                    dec = jnp.exp(jnp.where(tril, seg, NEG_BIG))
                    mm = (dec * cb).astype(BF16)
                    part = _dot(mm, jnp.where(keep, xdt, 0.0).astype(BF16))
                    y = part if y is None else y + part
                st = state_ref[j]
                y = y + _dot(cg, st.astype(BF16)) * jnp.exp(pair_pattern(acum, h0))
                dte = jnp.exp(pair_pattern(a_last - acum, h0))
                contrib = _dot(bg_t, (xdt * dte).astype(BF16))
                state_ref[j] = st * jnp.exp(pair_pattern(a_last, h0)) + contrib
                ys.append(y + dskip_ref[:, j * LANES:(j + 1) * LANES] * xs)
        yv = jnp.concatenate(ys, axis=-1)
        z = z_ref[rows, :].astype(F32)
        vv = yv * _silu(z)
        gwid = gw // SSD_GROUPS
        outs = []
        for g in range(SSD_GROUPS):
            vg = vv[:, g * gwid:(g + 1) * gwid]
            outs.append(vg * lax.rsqrt(jnp.mean(vg * vg, axis=-1, keepdims=True) + EPS))
        o_ref[rows, 0:gw] = (jnp.concatenate(outs, axis=-1) * ng_ref[...]).astype(BF16)
        return carry

    lax.fori_loop(0, tc // L, chunk_body, 0)


def _ssd_sconv(proj, dt_raw, cw, cb, dtb, alog, dskip, ng, sw, batch, seq):
    n = proj.shape[0]
    tc = _pick(seq, 512)
    nt = seq // tc
    row = lambda b, t: b * nt + t
    const = lambda b, t: (0, 0)
    return pl.pallas_call(
        functools.partial(_ssd_kernel, tc=tc),
        out_shape=jax.ShapeDtypeStruct((n, 2 * GROUP_WIDTH), BF16),
        grid=(batch, nt),
        in_specs=[
            pl.BlockSpec((tc, GROUP_WIDTH), lambda b, t: (row(b, t), COL_Z // GROUP_WIDTH)),
            pl.BlockSpec((tc, SSD_XBC), lambda b, t: (row(b, t), COL_XBC // SSD_XBC)),
            pl.BlockSpec((tc, GROUP_WIDTH), lambda b, t: (row(b, t), COL_GB // GROUP_WIDTH)),
            pl.BlockSpec((tc, GROUP_WIDTH), lambda b, t: (row(b, t), COL_GC // GROUP_WIDTH)),
            pl.BlockSpec((tc, GROUP_WIDTH), lambda b, t: (row(b, t), COL_U // GROUP_WIDTH)),
            pl.BlockSpec((tc, LANES), lambda b, t: (row(b, t), 0)),
            pl.BlockSpec((SSD_CONV, SSD_XBC), const),
            pl.BlockSpec((1, SSD_XBC), const),
            pl.BlockSpec((1, LANES), const),
            pl.BlockSpec((1, LANES), const),
            pl.BlockSpec((1, GROUP_WIDTH), const),
            pl.BlockSpec((1, GROUP_WIDTH), const),
            pl.BlockSpec((SC_CONV, GROUP_WIDTH), const),
        ],
        out_specs=pl.BlockSpec((tc, 2 * GROUP_WIDTH), lambda b, t: (row(b, t), 0)),
        scratch_shapes=[
            pltpu.VMEM((SUBLANES, SSD_XBC), F32),
            pltpu.VMEM((SUBLANES, GROUP_WIDTH), F32),
            pltpu.VMEM((SSD_HEADS // 2, SSD_STATE, LANES), F32),
            pltpu.VMEM((tc, GROUP_WIDTH), F32),
            pltpu.VMEM((tc, SSD_GROUPS * SSD_STATE), F32),
            pltpu.VMEM((tc, SSD_GROUPS * SSD_STATE), F32),
        ],
        compiler_params=_params(("parallel", "arbitrary")),
        name="ssd_sconv",
    )(proj, proj, proj, proj, proj, dt_raw, cw, cb, dtb, alog, dskip, ng, sw)


def _outproj_kernel(a_ref, sc_ref, w_ref, x_ref, mod_ref, g_ref, o_ref):
    gw = GROUP_WIDTH
    y = _dot(a_ref[...], w_ref[0:gw, :]) + _dot(sc_ref[...], w_ref[gw:, :])
    o_ref[...] = x_ref[...] + mod_ref[2:3, :] * _rms(y, g_ref[...])


def _outproj(attn, sc, w, x2, mod_l, g, rows_per_batch):
    n, d = x2.shape
    tm = _pick(rows_per_batch, 512)
    bpb = rows_per_batch // tm
    return pl.pallas_call(
        _outproj_kernel,
        out_shape=jax.ShapeDtypeStruct((n, d), F32),
        grid=(n // tm,),
        in_specs=[
            pl.BlockSpec((tm, GROUP_WIDTH), lambda i: (i, 0)),
            pl.BlockSpec((tm, 2 * GROUP_WIDTH), lambda i: (i, 0)),
            pl.BlockSpec((3 * GROUP_WIDTH, d), lambda i: (0, 0)),
            pl.BlockSpec((tm, d), lambda i: (i, 0)),
            pl.BlockSpec((None, 6, d), lambda i: (i // bpb, 0, 0)),
            pl.BlockSpec((1, d), lambda i: (0, 0)),
        ],
        out_specs=pl.BlockSpec((tm, d), lambda i: (i, 0)),
        compiler_params=_params(("parallel",)),
        name="outproj",
    )(attn, sc, w, x2, mod_l, g)


def _ffn_kernel(x_ref, mod_ref, gpre_ref, gpost_ref, w1_ref, w3_ref, w2_ref, o_ref, h_ref, acc_ref):
    j = pl.program_id(1)

    @pl.when(j == 0)
    def _():
        h = _rms(x_ref[...], gpre_ref[...]) * (1.0 + mod_ref[4:5, :]) + mod_ref[3:4, :]
        h_ref[...] = h.astype(BF16)

    h = h_ref[...]
    t = (_silu(_dot(h, w1_ref[...])) * _dot(h, w3_ref[...])).astype(BF16)
    part = _dot(t, w2_ref[...])

    @pl.when(j == 0)
    def _():
        acc_ref[...] = part

    @pl.when(j > 0)
    def _():
        acc_ref[...] += part

    @pl.when(j == pl.num_programs(1) - 1)
    def _():
        o_ref[...] = x_ref[...] + mod_ref[5:6, :] * _rms(acc_ref[...], gpost_ref[...])


def _ffn(x2, mod_l, gpre, gpost, w1, w3, w2, rows_per_batch):
    n, d = x2.shape
    dff = w1.shape[1]
    tm = _pick(rows_per_batch, 512)
    tf = dff // 2 if (dff // 2) % LANES == 0 else dff
    bpb = rows_per_batch // tm
    return pl.pallas_call(
        _ffn_kernel,
        out_shape=jax.ShapeDtypeStruct((n, d), F32),
        grid=(n // tm, dff // tf),
        in_specs=[
            pl.BlockSpec((tm, d), lambda i, j: (i, 0)),
            pl.BlockSpec((None, 6, d), lambda i, j: (i // bpb, 0, 0)),
            pl.BlockSpec((1, d), lambda i, j: (0, 0)),
            pl.BlockSpec((1, d), lambda i, j: (0, 0)),
            pl.BlockSpec((d, tf), lambda i, j: (0, j)),
            pl.BlockSpec((d, tf), lambda i, j: (0, j)),
            pl.BlockSpec((tf, d), lambda i, j: (j, 0)),
        ],
        out_specs=pl.BlockSpec((tm, d), lambda i, j: (i, 0)),
        scratch_shapes=[pltpu.VMEM((tm, d), BF16), pltpu.VMEM((tm, d), F32)],
        compiler_params=_params(("parallel", "arbitrary")),
        name="dense_ffn",
    )(x2, mod_l, gpre, gpost, w1, w3, w2)


MOE_TM = 512
MOE_WIN = 512
CMB_T = 256
CMB_WIN = CMB_T + BF16_ROWS
CMB_K = 512


def _router_kernel(x_ref, mod_ref, g_ref, wr_ref, h_ref, route_ref):
    h = _rms(x_ref[...], g_ref[...]) * (1.0 + mod_ref[4:5, :]) + mod_ref[3:4, :]
    hh, hl = _split2(h)
    h_ref[...] = hh
    a = _dot(hh, wr_ref[...])
    b = _dot(hl, wr_ref[...])
    ne = N_EXPERTS
    logits = a + b + pltpu.roll(a, LANES - ne, 1)
    lane = lax.broadcasted_iota(I32, logits.shape, 1)
    valid = lane < ne
    l1 = jnp.max(jnp.where(valid, logits, NEG_BIG), axis=-1, keepdims=True)
    e1 = jnp.min(jnp.where(valid & (logits == l1), lane, LANES), axis=-1, keepdims=True)
    rest = valid & (lane != e1)
    l2 = jnp.max(jnp.where(rest, logits, NEG_BIG), axis=-1, keepdims=True)
    e2 = jnp.min(jnp.where(rest & (logits == l2), lane, LANES), axis=-1, keepdims=True)
    w = jnp.exp(l2 - l1)
    g1 = 1.0 / (1.0 + w)
    g2 = w / (1.0 + w)
    lane8 = lax.broadcasted_iota(I32, route_ref.shape, 1)
    route_ref[...] = jnp.where(lane8 == 0, e1.astype(F32),
                               jnp.where(lane8 == 1, e2.astype(F32),
                                         jnp.where(lane8 == 2, g1, jnp.where(lane8 == 3, g2, 0.0))))


def _router(x2, mod_l, g, wr_packed, rows_per_batch):
    n, d = x2.shape
    tm = _pick(rows_per_batch, 512)
    bpb = rows_per_batch // tm
    return pl.pallas_call(
        _router_kernel,
        out_shape=(jax.ShapeDtypeStruct((n, d), BF16), jax.ShapeDtypeStruct((n, SUBLANES), F32)),
        grid=(n // tm,),
        in_specs=[
            pl.BlockSpec((tm, d), lambda i: (i, 0)),
            pl.BlockSpec((None, 6, d), lambda i: (i // bpb, 0, 0)),
            pl.BlockSpec((1, d), lambda i: (0, 0)),
            pl.BlockSpec((d, LANES), lambda i: (0, 0)),
        ],
        out_specs=(pl.BlockSpec((tm, d), lambda i: (i, 0)),
                   pl.BlockSpec((tm, SUBLANES), lambda i: (i, 0))),
        compiler_params=_params(("parallel",)),
        name="moe_router",
    )(x2, mod_l, g, wr_packed)


def _dispatch_kernel(w0_ref, nw_ref, h_hbm, pos_ref, gt_ref, xb_ref, gate_ref, buf_ref, sem_ref, acc_ref, gacc_ref):
    b = pl.program_id(0)
    tm, win = MOE_TM, MOE_WIN
    w0 = w0_ref[b]
    nw = nw_ref[b]
    p0 = b * tm

    def copy(w, slot):
        return pltpu.make_async_copy(h_hbm.at[pl.ds(pl.multiple_of(w * win, win), win), :],
                                     buf_ref.at[slot], sem_ref.at[slot])

    acc_ref[...] = jnp.zeros(acc_ref.shape, F32)
    gacc_ref[...] = jnp.zeros(gacc_ref.shape, F32)

    @pl.when(nw > 0)
    def _():
        copy(w0, 0).start()

    slot_id = p0 + lax.broadcasted_iota(I32, (tm, win), 0)

    def body(k, carry):
        slot = k % 2
        copy(w0 + k, slot).wait()

        @pl.when(k + 1 < nw)
        def _():
            copy(w0 + k + 1, 1 - slot).start()

        pos = pos_ref[w0 + k]
        gt = gt_ref[w0 + k]
        eq0 = pos[0:1, :] == slot_id
        eq1 = pos[1:2, :] == slot_id
        onehot = jnp.where(eq0 | eq1, 1.0, 0.0).astype(BF16)
        acc_ref[...] += _dot(onehot, buf_ref[slot])
        gsel = jnp.where(eq0, gt[0:1, :], 0.0) + jnp.where(eq1, gt[1:2, :], 0.0)
        gacc_ref[...] += jnp.sum(gsel, axis=-1, keepdims=True)
        return carry

    lax.fori_loop(0, nw, body, 0)
    xb_ref[...] = acc_ref[...].astype(BF16)
    gate_ref[...] = gacc_ref[...]


def _dispatch(win0, nwin, h, pos_w, gate_w, n_blocks):
    n, d = h.shape
    tm, win = MOE_TM, MOE_WIN
    nwt = n // win
    return pl.pallas_call(
        _dispatch_kernel,
        out_shape=(jax.ShapeDtypeStruct((n_blocks * tm, d), BF16),
                   jax.ShapeDtypeStruct((n_blocks * tm, 1), F32)),
        grid_spec=pltpu.PrefetchScalarGridSpec(
            num_scalar_prefetch=2,
            grid=(n_blocks,),
            in_specs=[
                pl.BlockSpec(memory_space=pl.ANY),
                pl.BlockSpec((nwt, 2, win), lambda b, w0, nw: (0, 0, 0)),
                pl.BlockSpec((nwt, 2, win), lambda b, w0, nw: (0, 0, 0)),
            ],
            out_specs=(pl.BlockSpec((tm, d), lambda b, w0, nw: (b, 0)),
                       pl.BlockSpec((tm, 1), lambda b, w0, nw: (b, 0))),
            scratch_shapes=[
                pltpu.VMEM((2, win, d), BF16),
                pltpu.SemaphoreType.DMA((2,)),
                pltpu.VMEM((tm, d), F32),
                pltpu.VMEM((tm, 1), F32),
            ],
        ),
        compiler_params=_params(("arbitrary",)),
        name="moe_dispatch",
    )(win0, nwin, h, pos_w, gate_w)


def _expert_kernel(be_ref, used_ref, xb_ref, gate_ref, w1_ref, w3_ref, w2_ref, yb_ref, acc_ref):
    b = pl.program_id(0)
    j = pl.program_id(1)
    last = pl.num_programs(1) - 1
    live = b < used_ref[0]

    @pl.when(live)
    def _():
        xb = xb_ref[...]
        t = (_silu(_dot(xb, w1_ref[...])) * _dot(xb, w3_ref[...])).astype(BF16)
        part = _dot(t, w2_ref[...])

        @pl.when(j == 0)
        def _():
            acc_ref[...] = part

        @pl.when(j > 0)
        def _():
            acc_ref[...] += part

        @pl.when(j == last)
        def _():
            yb_ref[...] = (acc_ref[...] * gate_ref[...]).astype(BF16)

    @pl.when(jnp.logical_not(live) & (j == last))
    def _():
        yb_ref[...] = jnp.zeros(yb_ref.shape, BF16)


def _experts(block_expert, used, xb, gate_buf, w1, w3, w2, n_blocks):
    d = xb.shape[1]
    dff = w1.shape[2]
    tm = MOE_TM
    tf = dff // 2 if (dff // 2) % LANES == 0 else dff
    nj = dff // tf

    def jj(b, j, be, used):
        return jnp.where(b < used[0], j, nj - 1)

    return pl.pallas_call(
        _expert_kernel,
        out_shape=jax.ShapeDtypeStruct((n_blocks * tm, d), BF16),
        grid_spec=pltpu.PrefetchScalarGridSpec(
            num_scalar_prefetch=2,
            grid=(n_blocks, nj),
            in_specs=[
                pl.BlockSpec((tm, d), lambda b, j, be, used: (b, 0)),
                pl.BlockSpec((tm, 1), lambda b, j, be, used: (b, 0)),
                pl.BlockSpec((None, d, tf), lambda b, j, be, used: (be[b], 0, jj(b, j, be, used))),
                pl.BlockSpec((None, d, tf), lambda b, j, be, used: (be[b], 0, jj(b, j, be, used))),
                pl.BlockSpec((None, tf, d), lambda b, j, be, used: (be[b], jj(b, j, be, used), 0)),
            ],
            out_specs=pl.BlockSpec((tm, d), lambda b, j, be, used: (b, 0)),
            scratch_shapes=[pltpu.VMEM((tm, d), F32)],
        ),
        compiler_params=_params(("arbitrary", "arbitrary")),
        name="moe_experts",
    )(block_expert, used, xb, gate_buf, w1, w3, w2)


def _combine_kernel(ws_ref, yb_hbm, pos_ref, x_ref, mod_ref, g_ref, o_ref, buf_ref, sem_ref):
    i = pl.program_id(0)
    n_steps = pl.num_programs(0)
    ne = N_EXPERTS
    t, cw, ck = CMB_T, CMB_WIN, CMB_K

    def copy(step, e, slot):
        start = pl.multiple_of(ws_ref[step * ne + e], BF16_ROWS)
        return pltpu.make_async_copy(yb_hbm.at[pl.ds(start, cw), :],
                                     buf_ref.at[slot, e, 0:cw, :], sem_ref.at[slot, e])

    @pl.when(i == 0)
    def _():
        buf_ref[...] = jnp.zeros(buf_ref.shape, BF16)
        for e in range(ne):
            copy(0, e, 0).start()

    slot = i % 2

    @pl.when(i + 1 < n_steps)
    def _():
        for e in range(ne):
            copy(i + 1, e, 1 - slot).start()

    pe = pos_ref[...]
    pos = pe[:, 0:TOP_K]
    top = pe[:, TOP_K:2 * TOP_K]
    col = lax.broadcasted_iota(I32, (t, ck), 1)
    y = jnp.zeros((t, x_ref.shape[1]), F32)
    for e in range(ne):
        copy(i, e, slot).wait()
        rel = jnp.where(top == e, pos - ws_ref[i * ne + e], -1)
        onehot = jnp.where((rel[:, 0:1] == col) | (rel[:, 1:2] == col), 1.0, 0.0).astype(BF16)
        y = y + _dot(onehot, buf_ref[slot, e])
    o_ref[...] = x_ref[...] + mod_ref[5:6, :] * _rms(y, g_ref[...])


def _combine(win_start, yb, pos, x2, mod_l, g, rows_per_batch):
    n, d = x2.shape
    t = CMB_T
    bpb = rows_per_batch // t
    return pl.pallas_call(
        _combine_kernel,
        out_shape=jax.ShapeDtypeStruct((n, d), F32),
        grid_spec=pltpu.PrefetchScalarGridSpec(
            num_scalar_prefetch=1,
            grid=(n // t,),
            in_specs=[
                pl.BlockSpec(memory_space=pl.ANY),
                pl.BlockSpec((t, 2 * TOP_K), lambda i, ws: (i, 0)),
                pl.BlockSpec((t, d), lambda i, ws: (i, 0)),
                pl.BlockSpec((None, 6, d), lambda i, ws: (i // bpb, 0, 0)),
                pl.BlockSpec((1, d), lambda i, ws: (0, 0)),
            ],
            out_specs=pl.BlockSpec((t, d), lambda i, ws: (i, 0)),
            scratch_shapes=[
                pltpu.VMEM((2, N_EXPERTS, CMB_K, d), BF16),
                pltpu.SemaphoreType.DMA((2, N_EXPERTS)),
            ],
        ),
        compiler_params=_params(("arbitrary",)),
        name="moe_combine",
    )(win_start, yb, pos, x2, mod_l, g)


def _moe(x2, mod_l, gpre, gpost, w_router, w1, w3, w2, rows_per_batch):
    n, d = x2.shape
    ne, tm, win, t = N_EXPERTS, MOE_TM, MOE_WIN, CMB_T
    wr_hi, wr_lo = _split2(w_router)
    wr_packed = jnp.zeros((d, LANES), BF16).at[:, 0:ne].set(wr_hi).at[:, ne:2 * ne].set(wr_lo)
    h, route = _router(x2, mod_l, gpre, wr_packed, rows_per_batch)

    top = route[:, 0:TOP_K].astype(I32)
    gates = route[:, TOP_K:2 * TOP_K]
    sel = jnp.sum((top[:, :, None] == jnp.arange(ne, dtype=I32)).astype(I32), axis=1)
    cum = jnp.cumsum(sel, axis=0)
    excl = cum - sel
    counts = cum[-1]
    padded = (counts + tm - 1) // tm * tm
    pend = jnp.cumsum(padded)
    pstart = pend - padded
    base = pstart[None, :] + excl
    pos = jnp.take_along_axis(base, top, axis=1)

    n_blocks = (n * TOP_K) // tm + ne + 1
    blk = jnp.arange(n_blocks, dtype=I32) * tm
    block_expert = jnp.minimum(jnp.searchsorted(pend, blk, side='right'), ne - 1).astype(I32)
    used = (pend[-1] // tm).astype(I32).reshape(1)
    live = blk < pend[-1]
    rank0 = blk - pstart[block_expert]
    rank1 = jnp.minimum(rank0 + tm, counts[block_expert])
    kmax = n // tm + 1
    ranks = jnp.arange(kmax, dtype=I32) * tm
    find = jax.vmap(lambda col, v: jnp.searchsorted(col, v, side='left'))
    cum_t = cum.T
    tok_of_rank = find(cum_t, jnp.broadcast_to(ranks + 1, (ne, kmax)))
    tok_of_last = find(cum_t, counts[:, None])[:, 0]
    k0 = rank0 // tm
    t_lo = tok_of_rank[block_expert, jnp.clip(k0, 0, kmax - 1)]
    t_hi = jnp.where(rank0 + tm < counts[block_expert],
                     tok_of_rank[block_expert, jnp.clip(k0 + 1, 0, kmax - 1)],
                     tok_of_last[block_expert])
    has = live & (rank1 > rank0)
    win0 = jnp.where(has, t_lo // win, 0).astype(I32)
    nwin = jnp.where(has, t_hi // win - t_lo // win + 1, 0).astype(I32)

    pos_w = pos.T.reshape(TOP_K, n // win, win).transpose(1, 0, 2)
    gate_w = gates.T.reshape(TOP_K, n // win, win).transpose(1, 0, 2)
    xb, gate_buf = _dispatch(win0, nwin, h, pos_w, gate_w, n_blocks)
    yb = _experts(block_expert, used, xb, gate_buf, w1, w3, w2, n_blocks)

    win_start = (base[::t] // BF16_ROWS * BF16_ROWS).astype(I32).reshape(-1)
    return _combine(win_start, yb, jnp.concatenate([pos, top], axis=1), x2, mod_l, gpost, rows_per_batch)


def _reorder_w_in(w):
    main = jnp.concatenate([w[:, 0:3072], w[:, 3080:4616]], axis=1).astype(BF16)
    wdt = jnp.zeros((w.shape[0], LANES), BF16).at[:, 0:SSD_HEADS].set(w[:, 3072:3080].astype(BF16))
    return main, wdt


def _pad_row(v, width=LANES):
    return jnp.zeros((1, width), F32).at[0, 0:v.shape[0]].set(v)


def kernel(x, c, w_mod, b_mod, norm_mix_pre, norm_mix_post, norm_ffn_pre, norm_ffn_post, w_in, w_out, lambda_qk, attn_subln, ssd_conv_w, ssd_conv_b, ssd_dt_bias, ssd_a_log, ssd_d, ssd_norm, sconv_w, ffn_w1, ffn_w3, ffn_w2, moe_router, moe_w1, moe_w3, moe_w2):
    batch, seq, d = x.shape
    depth = w_mod.shape[0]
    n = batch * seq
    assert seq % SSD_CHUNK == 0 and w_in.shape[2] == PROJ_COLS + SSD_HEADS
    mod = _modulation(c, w_mod, b_mod).reshape(depth, batch, 6, d)
    x2 = x.reshape(n, d)
    for i in range(depth):
        mod_l = mod[i]
        w_main, w_dt = _reorder_w_in(w_in[i])
        proj, dt_raw = _inproj(x2, mod_l, norm_mix_pre[i][None, :], w_main, w_dt, seq)
        lam_init = 0.8 - 0.6 * math.exp(-0.3 * i)
        attn = _attention(proj, lambda_qk[i], attn_subln[i][None, :], batch, seq, lam_init)
        sc = _ssd_sconv(proj, dt_raw, ssd_conv_w[i], ssd_conv_b[i][None, :], _pad_row(ssd_dt_bias[i]),
                        _pad_row(ssd_a_log[i]), jnp.repeat(ssd_d[i], SSD_HEAD_DIM)[None, :],
                        ssd_norm[i][None, :], sconv_w[i], batch, seq)
        x2 = _outproj(attn, sc, w_out[i].astype(BF16), x2, mod_l, norm_mix_post[i][None, :], seq)
        if i % 2 == 0:
            x2 = _ffn(x2, mod_l, norm_ffn_pre[i][None, :], norm_ffn_post[i][None, :],
                      ffn_w1[i // 2].astype(BF16), ffn_w3[i // 2].astype(BF16), ffn_w2[i // 2].astype(BF16), seq)
        else:
            x2 = _moe(x2, mod_l, norm_ffn_pre[i][None, :], norm_ffn_post[i][None, :], moe_router[i // 2],
                      moe_w1[i // 2].astype(BF16), moe_w3[i // 2].astype(BF16), moe_w2[i // 2].astype(BF16), seq)
    return x2.reshape(batch, seq, d)
```

```python
import functools
import math

import jax
import jax.numpy as jnp
from jax import lax
from jax.experimental import pallas as pl
from jax.experimental.pallas import tpu as pltpu

F32 = jnp.float32
BF16 = jnp.bfloat16
I32 = jnp.int32

EPS = 1e-6
GROUP_WIDTH = 512
ATTN_HEADS = 4
ATTN_HEAD_DIM = 64
SSD_HEAD_DIM = 64
SSD_HEADS = 8
SSD_GROUPS = 2
SSD_STATE = 128
SSD_CONV = 4
SSD_CHUNK = 128
SSD_XBC = GROUP_WIDTH + 2 * SSD_GROUPS * SSD_STATE
SC_CONV = 3
N_EXPERTS = 8
TOP_K = 2

LANES = 128
SUBLANES = 8
BF16_ROWS = 16
VMEM_LIMIT = 48 * 1024 * 1024
NEG_BIG = -1e30

COL_Q, COL_K, COL_V, COL_Z, COL_XBC, COL_GB, COL_GC, COL_U = 0, 512, 1024, 1536, 2048, 3072, 3584, 4096
PROJ_COLS = 4608

NT_DIMS = (((1,), (1,)), ((), ()))


def _dot(a, b):
    return jnp.dot(a, b, preferred_element_type=F32)


def _dot_nt(a, b):
    return lax.dot_general(a, b, NT_DIMS, preferred_element_type=F32)


def _split2(x):
    hi = x.astype(BF16)
    lo = (x - hi.astype(F32)).astype(BF16)
    return hi, lo


def _split3(x):
    hi = x.astype(BF16)
    r = x - hi.astype(F32)
    mid = r.astype(BF16)
    lo = (r - mid.astype(F32)).astype(BF16)
    return hi, mid, lo


def _sigmoid(x):
    return 1.0 / (1.0 + jnp.exp(-x))


def _silu(x):
    return x * _sigmoid(x)


def _softplus(x):
    return jnp.maximum(x, 0.0) + jnp.log(1.0 + jnp.exp(-jnp.abs(x)))


def _rms(x, g):
    return x * lax.rsqrt(jnp.mean(x * x, axis=-1, keepdims=True) + EPS) * g


def _params(sem):
    return pltpu.CompilerParams(dimension_semantics=sem, vmem_limit_bytes=VMEM_LIMIT)


def _pick(n, pref):
    t = min(n, pref)
    assert n % t == 0, (n, pref)
    return t


def _mod_kernel(c_ref, w_ref, b_ref, o_ref):
    c = c_ref[...]
    ah, al = _split2(_silu(c))
    wh, wl = _split2(w_ref[...])
    o_ref[...] = _dot(ah, wh) + _dot(ah, wl) + _dot(al, wh) + b_ref[...]


def _modulation(c, w_mod, b_mod):
    depth, d, n6 = w_mod.shape
    b = c.shape[0]
    tn = _pick(n6, 1536)
    return pl.pallas_call(
        _mod_kernel,
        out_shape=jax.ShapeDtypeStruct((depth, b, n6), F32),
        grid=(depth, n6 // tn),
        in_specs=[
            pl.BlockSpec((b, d), lambda l, j: (0, 0)),
            pl.BlockSpec((None, d, tn), lambda l, j: (l, 0, j)),
            pl.BlockSpec((None, 1, tn), lambda l, j: (l, 0, j)),
        ],
        out_specs=pl.BlockSpec((None, b, tn), lambda l, j: (l, 0, j)),
        compiler_params=_params(("parallel", "parallel")),
        name="modulation",
    )(c, w_mod, b_mod.reshape(depth, 1, n6))


def _inproj_kernel(x_ref, mod_ref, g_ref, w_ref, wdt_ref, o_ref, dt_ref, h_ref):
    @pl.when(pl.program_id(1) == 0)
    def _():
        h = _rms(x_ref[...], g_ref[...]) * (1.0 + mod_ref[1:2, :]) + mod_ref[0:1, :]
        hb = h.astype(BF16)
        h_ref[...] = hb
        dt_ref[...] = _dot(hb, wdt_ref[...])

    o_ref[...] = _dot(h_ref[...], w_ref[...]).astype(BF16)


def _inproj(x2, mod_l, g, w, wdt, rows_per_batch):
    n, d = x2.shape
    tm = _pick(rows_per_batch, 1024)
    tn = 1536
    bpb = rows_per_batch // tm
    return pl.pallas_call(
        _inproj_kernel,
        out_shape=(jax.ShapeDtypeStruct((n, PROJ_COLS), BF16),
                   jax.ShapeDtypeStruct((n, LANES), F32)),
        grid=(n // tm, PROJ_COLS // tn),
        in_specs=[
            pl.BlockSpec((tm, d), lambda i, j: (i, 0)),
            pl.BlockSpec((None, 6, d), lambda i, j: (i // bpb, 0, 0)),
            pl.BlockSpec((1, d), lambda i, j: (0, 0)),
            pl.BlockSpec((d, tn), lambda i, j: (0, j)),
            pl.BlockSpec((d, LANES), lambda i, j: (0, 0)),
        ],
        out_specs=(pl.BlockSpec((tm, tn), lambda i, j: (i, j)),
                   pl.BlockSpec((tm, LANES), lambda i, j: (i, 0))),
        scratch_shapes=[pltpu.VMEM((tm, d), BF16)],
        compiler_params=_params(("parallel", "arbitrary")),
        name="inproj",
    )(x2, mod_l, g, w, wdt)


def _attn_kernel(q_ref, k_ref, v_ref, lq_ref, sub_ref, o_ref, qt_ref, vt_ref, m_ref, l_ref, acc_ref,
                 s_ref, cmax_ref, *, tq, lam_init):
    i = pl.program_id(2)
    d = ATTN_HEAD_DIM
    n_chunks = vt_ref.shape[0]

    @pl.when(i == 0)
    def _():
        for cidx in range(n_chunks):
            vt_ref[cidx] = v_ref[cidx * tq:(cidx + 1) * tq, :].astype(F32).T.astype(BF16)

    qt = (q_ref[...].astype(F32) * (d ** -0.5 * math.log2(math.e))).T
    row = lax.broadcasted_iota(I32, qt.shape, 0)
    qt_ref[:, 0:tq] = jnp.where(row < d, qt, 0.0).astype(BF16)
    qt_ref[:, tq:] = jnp.where(row >= d, qt, 0.0).astype(BF16)
    m_ref[...] = jnp.full(m_ref.shape, NEG_BIG, F32)
    l_ref[...] = jnp.zeros(l_ref.shape, F32)
    acc_ref[...] = jnp.zeros(acc_ref.shape, F32)

    def scores(j, slot, masked):
        start = pl.multiple_of(j * tq, tq)
        s = _dot(k_ref[pl.ds(start, tq), :], qt_ref[...])
        if masked:
            kk = lax.broadcasted_iota(I32, (tq, tq), 0)
            qq = lax.broadcasted_iota(I32, (tq, tq), 1)
            keep = kk <= qq
            s = jnp.where(jnp.concatenate([keep, keep], axis=1), s, NEG_BIG)
        s_ref[slot] = s
        cmax_ref[slot] = jnp.max(s, axis=0, keepdims=True)

    def accumulate(j, slot):
        m_prev = m_ref[...]
        m_new = jnp.maximum(m_prev, cmax_ref[slot])
        alpha = jnp.exp2(m_prev - m_new)
        p = jnp.exp2(s_ref[slot] - m_new)
        l_ref[...] = alpha * l_ref[...] + jnp.sum(p, axis=0, keepdims=True)
        acc_ref[...] = alpha * acc_ref[...] + _dot(vt_ref[j], p.astype(BF16))
        m_ref[...] = m_new

    scores(i, 0, True)

    def step(t, slot):
        scores(t, 1 - slot, False)
        accumulate(jnp.where(t == 0, i, t - 1), slot)

    def body(u, carry):
        step(2 * u, 0)
        step(2 * u + 1, 1)
        return carry

    lax.fori_loop(0, i // 2, body, 0)

    @pl.when(i % 2 == 1)
    def _():
        step(i - 1, 0)
        accumulate(i - 1, 1)

    @pl.when(i % 2 == 0)
    def _():
        accumulate(jnp.where(i == 0, i, i - 1), 0)

    lq = lq_ref[...]
    lam = (jnp.exp(jnp.sum(lq[0:1, :] * lq[1:2, :], axis=-1, keepdims=True))
           - jnp.exp(jnp.sum(lq[2:3, :] * lq[3:4, :], axis=-1, keepdims=True)) + lam_init)
    inv = 1.0 / l_ref[...]
    ot = acc_ref[:, 0:tq] * inv[:, 0:tq] - lam * (acc_ref[:, tq:] * inv[:, tq:])
    ot = ot * lax.rsqrt(jnp.mean(ot * ot, axis=0, keepdims=True) + EPS)
    o_ref[...] = (ot.T * sub_ref[...] * (1.0 - lam_init)).astype(BF16)


def _attention(proj, lambda_qk, subln, batch, seq, lam_init):
    n = proj.shape[0]
    tq = _pick(seq, 512)
    nq = seq // tq
    h = ATTN_HEADS
    return pl.pallas_call(
        functools.partial(_attn_kernel, tq=tq, lam_init=lam_init),
        out_shape=jax.ShapeDtypeStruct((n, GROUP_WIDTH), BF16),
        grid=(batch, h, nq),
        in_specs=[
            pl.BlockSpec((tq, LANES), lambda b, hh, i: (b * nq + i, COL_Q // LANES + hh)),
            pl.BlockSpec((seq, LANES), lambda b, hh, i: (b, COL_K // LANES + hh)),
            pl.BlockSpec((seq, LANES), lambda b, hh, i: (b, COL_V // LANES + hh)),
            pl.BlockSpec((4, ATTN_HEAD_DIM), lambda b, hh, i: (0, 0)),
            pl.BlockSpec((1, LANES), lambda b, hh, i: (0, 0)),
        ],
        out_specs=pl.BlockSpec((tq, LANES), lambda b, hh, i: (b * nq + i, hh)),
        scratch_shapes=[
            pltpu.VMEM((LANES, 2 * tq), BF16),
            pltpu.VMEM((nq, LANES, tq), BF16),
            pltpu.VMEM((1, 2 * tq), F32),
            pltpu.VMEM((1, 2 * tq), F32),
            pltpu.VMEM((LANES, 2 * tq), F32),
            pltpu.VMEM((2, tq, 2 * tq), F32),
            pltpu.VMEM((2, 1, 2 * tq), F32),
        ],
        compiler_params=_params(("parallel", "parallel", "arbitrary")),
        name="diff_attention",
    )(proj, proj, proj, lambda_qk, subln)


def _causal_conv(x, carry, w, width):
    row = lax.broadcasted_iota(I32, carry.shape, 0)
    out = x * w[width - 1:width, :]
    for k in range(1, width):
        xr = pltpu.roll(x, k, 0)
        cr = pltpu.roll(carry, k, 0)
        head = jnp.where(row < k, cr, xr[0:SUBLANES, :])
        xk = jnp.concatenate([head, xr[SUBLANES:, :]], axis=0)
        out = out + xk * w[width - 1 - k:width - k, :]
    return out


def _ssd_kernel(z_ref, xbc_ref, gb_ref, gc_ref, u_ref, dt_ref, cw_ref, cb_ref, dtb_ref, alog_ref,
                dskip_ref, ng_ref, sw_ref, o_ref,
                cx_ref, cs_ref, state_ref, xs_ref, bm_ref, cm_ref, *, tc):
    L = SSD_CHUNK
    gw = GROUP_WIDTH
    ns = SSD_STATE

    @pl.when(pl.program_id(1) == 0)
    def _():
        cx_ref[...] = jnp.zeros(cx_ref.shape, F32)
        cs_ref[...] = jnp.zeros(cs_ref.shape, F32)
        state_ref[...] = jnp.zeros(state_ref.shape, F32)

    pu = gc_ref[...].astype(F32) * u_ref[...].astype(F32)
    sconv = gb_ref[...].astype(F32) * _causal_conv(pu, cs_ref[...], sw_ref[...], SC_CONV)
    cs_ref[...] = pu[tc - SUBLANES:, :]
    o_ref[:, gw:] = sconv.astype(BF16)

    xbc = xbc_ref[...].astype(F32)
    act = _silu(_causal_conv(xbc, cx_ref[...], cw_ref[...], SSD_CONV) + cb_ref[...])
    cx_ref[...] = xbc[tc - SUBLANES:, :]
    xs_ref[...] = act[:, 0:gw]
    bm_ref[...] = act[:, gw:gw + SSD_GROUPS * ns]
    cm_ref[...] = act[:, gw + SSD_GROUPS * ns:]

    lane = lax.broadcasted_iota(I32, (L, LANES), 1)
    lo = lane < SSD_HEAD_DIM
    rr = lax.broadcasted_iota(I32, (L, L), 0)
    cc = lax.broadcasted_iota(I32, (L, L), 1)
    tril = cc <= rr
    ltri = jnp.where(tril, 1.0, 0.0).astype(BF16)
    head_lane = lane < SSD_HEADS
    a_row = -jnp.exp(alog_ref[...])

    def pair_pattern(mat, h0):
        lo_b = lo[0:mat.shape[0], :]
        return jnp.where(lo_b, mat[:, h0:h0 + 1], mat[:, h0 + 1:h0 + 2])

    def chunk_body(c, carry):
        r0 = pl.multiple_of(c * L, L)
        rows = pl.ds(r0, L)
        dt = _softplus(dt_ref[rows, :] + dtb_ref[...])
        da = jnp.where(head_lane, dt * a_row, 0.0)
        d_hi, d_mid, d_lo = _split3(da)
        acum = _dot(ltri, d_hi) + _dot(ltri, d_mid) + _dot(ltri, d_lo)
        acum_t = acum.T
        a_last = acum[L - 1:L, :]
        ys = []
        for g in range(SSD_GROUPS):
            bg = bm_ref[rows, g * ns:(g + 1) * ns]
            cg = cm_ref[rows, g * ns:(g + 1) * ns].astype(BF16)
            cb = _dot_nt(cg, bg.astype(BF16))
            bg_t = bg.T.astype(BF16)
            for jp in range(SSD_HEADS // SSD_GROUPS // 2):
                j = g * (SSD_HEADS // SSD_GROUPS // 2) + jp
                h0 = 2 * j
                xs = xs_ref[rows, j * LANES:(j + 1) * LANES]
                xdt = xs * pair_pattern(dt, h0)
                y = None
                for hh, keep in ((h0, lo), (h0 + 1, jnp.logical_not(lo))):
                    seg = acum[:, hh:hh + 1] - acum_t[hh:hh + 1, :]
                    dec = jnp.exp(jnp.where(tril, seg, NEG_BIG))
                    mm = (dec * cb).astype(BF16)
                    part = _dot(mm, jnp.where(keep, xdt, 0.0).astype(BF16))
                    y = part if y is None else y + part
                st = state_ref[j]
                y = y + _dot(cg, st.astype(BF16)) * jnp.exp(pair_pattern(acum, h0))
                dte = jnp.exp(pair_pattern(a_last - acum, h0))
                contrib = _dot(bg_t, (xdt * dte).astype(BF16))
                state_ref[j] = st * jnp.exp(pair_pattern(a_last, h0)) + contrib
                ys.append(y + dskip_ref[:, j * LANES:(j + 1) * LANES] * xs)
        yv = jnp.concatenate(ys, axis=-1)
        z = z_ref[rows, :].astype(F32)
        vv = yv * _silu(z)
        gwid = gw // SSD_GROUPS
        outs = []
        for g in range(SSD_GROUPS):
            vg = vv[:, g * gwid:(g + 1) * gwid]
            outs.append(vg * lax.rsqrt(jnp.mean(vg * vg, axis=-1, keepdims=True) + EPS))
        o_ref[rows, 0:gw] = (jnp.concatenate(outs, axis=-1) * ng_ref[...]).astype(BF16)
        return carry

    lax.fori_loop(0, tc // L, chunk_body, 0)


def _ssd_sconv(proj, dt_raw, cw, cb, dtb, alog, dskip, ng, sw, batch, seq):
    n = proj.shape[0]
    tc = _pick(seq, 512)
    nt = seq // tc
    row = lambda b, t: b * nt + t
    const = lambda b, t: (0, 0)
    return pl.pallas_call(
        functools.partial(_ssd_kernel, tc=tc),
        out_shape=jax.ShapeDtypeStruct((n, 2 * GROUP_WIDTH), BF16),
        grid=(batch, nt),
        in_specs=[
            pl.BlockSpec((tc, GROUP_WIDTH), lambda b, t: (row(b, t), COL_Z // GROUP_WIDTH)),
            pl.BlockSpec((tc, SSD_XBC), lambda b, t: (row(b, t), COL_XBC // SSD_XBC)),
            pl.BlockSpec((tc, GROUP_WIDTH), lambda b, t: (row(b, t), COL_GB // GROUP_WIDTH)),
            pl.BlockSpec((tc, GROUP_WIDTH), lambda b, t: (row(b, t), COL_GC // GROUP_WIDTH)),
            pl.BlockSpec((tc, GROUP_WIDTH), lambda b, t: (row(b, t), COL_U // GROUP_WIDTH)),
            pl.BlockSpec((tc, LANES), lambda b, t: (row(b, t), 0)),
            pl.BlockSpec((SSD_CONV, SSD_XBC), const),
            pl.BlockSpec((1, SSD_XBC), const),
            pl.BlockSpec((1, LANES), const),
            pl.BlockSpec((1, LANES), const),
            pl.BlockSpec((1, GROUP_WIDTH), const),
            pl.BlockSpec((1, GROUP_WIDTH), const),
            pl.BlockSpec((SC_CONV, GROUP_WIDTH), const),
        ],
        out_specs=pl.BlockSpec((tc, 2 * GROUP_WIDTH), lambda b, t: (row(b, t), 0)),
        scratch_shapes=[
            pltpu.VMEM((SUBLANES, SSD_XBC), F32),
            pltpu.VMEM((SUBLANES, GROUP_WIDTH), F32),
            pltpu.VMEM((SSD_HEADS // 2, SSD_STATE, LANES), F32),
            pltpu.VMEM((tc, GROUP_WIDTH), F32),
            pltpu.VMEM((tc, SSD_GROUPS * SSD_STATE), F32),
            pltpu.VMEM((tc, SSD_GROUPS * SSD_STATE), F32),
        ],
        compiler_params=_params(("parallel", "arbitrary")),
        name="ssd_sconv",
    )(proj, proj, proj, proj, proj, dt_raw, cw, cb, dtb, alog, dskip, ng, sw)


def _outproj_kernel(a_ref, sc_ref, w_ref, x_ref, mod_ref, g_ref, o_ref):
    gw = GROUP_WIDTH
    y = _dot(a_ref[...], w_ref[0:gw, :]) + _dot(sc_ref[...], w_ref[gw:, :])
    o_ref[...] = x_ref[...] + mod_ref[2:3, :] * _rms(y, g_ref[...])


def _outproj(attn, sc, w, x2, mod_l, g, rows_per_batch):
    n, d = x2.shape
    tm = _pick(rows_per_batch, 512)
    bpb = rows_per_batch // tm
    return pl.pallas_call(
        _outproj_kernel,
        out_shape=jax.ShapeDtypeStruct((n, d), F32),
        grid=(n // tm,),
        in_specs=[
            pl.BlockSpec((tm, GROUP_WIDTH), lambda i: (i, 0)),
            pl.BlockSpec((tm, 2 * GROUP_WIDTH), lambda i: (i, 0)),
            pl.BlockSpec((3 * GROUP_WIDTH, d), lambda i: (0, 0)),
            pl.BlockSpec((tm, d), lambda i: (i, 0)),
            pl.BlockSpec((None, 6, d), lambda i: (i // bpb, 0, 0)),
            pl.BlockSpec((1, d), lambda i: (0, 0)),
        ],
        out_specs=pl.BlockSpec((tm, d), lambda i: (i, 0)),
        compiler_params=_params(("parallel",)),
        name="outproj",
    )(attn, sc, w, x2, mod_l, g)


def _ffn_kernel(x_ref, mod_ref, gpre_ref, gpost_ref, w1_ref, w3_ref, w2_ref, o_ref, h_ref, acc_ref):
    j = pl.program_id(1)

    @pl.when(j == 0)
    def _():
        h = _rms(x_ref[...], gpre_ref[...]) * (1.0 + mod_ref[4:5, :]) + mod_ref[3:4, :]
        h_ref[...] = h.astype(BF16)

    h = h_ref[...]
    t = (_silu(_dot(h, w1_ref[...])) * _dot(h, w3_ref[...])).astype(BF16)
    part = _dot(t, w2_ref[...])

    @pl.when(j == 0)
    def _():
        acc_ref[...] = part

    @pl.when(j > 0)
    def _():
        acc_ref[...] += part

    @pl.when(j == pl.num_programs(1) - 1)
    def _():
        o_ref[...] = x_ref[...] + mod_ref[5:6, :] * _rms(acc_ref[...], gpost_ref[...])


def _ffn(x2, mod_l, gpre, gpost, w1, w3, w2, rows_per_batch):
    n, d = x2.shape
    dff = w1.shape[1]
    tm = _pick(rows_per_batch, 512)
    tf = dff // 2 if (dff // 2) % LANES == 0 else dff
    bpb = rows_per_batch // tm
    return pl.pallas_call(
        _ffn_kernel,
        out_shape=jax.ShapeDtypeStruct((n, d), F32),
        grid=(n // tm, dff // tf),
        in_specs=[
            pl.BlockSpec((tm, d), lambda i, j: (i, 0)),
            pl.BlockSpec((None, 6, d), lambda i, j: (i // bpb, 0, 0)),
            pl.BlockSpec((1, d), lambda i, j: (0, 0)),
            pl.BlockSpec((1, d), lambda i, j: (0, 0)),
            pl.BlockSpec((d, tf), lambda i, j: (0, j)),
            pl.BlockSpec((d, tf), lambda i, j: (0, j)),
            pl.BlockSpec((tf, d), lambda i, j: (j, 0)),
        ],
        out_specs=pl.BlockSpec((tm, d), lambda i, j: (i, 0)),
        scratch_shapes=[pltpu.VMEM((tm, d), BF16), pltpu.VMEM((tm, d), F32)],
        compiler_params=_params(("parallel", "arbitrary")),
        name="dense_ffn",
    )(x2, mod_l, gpre, gpost, w1, w3, w2)


MOE_TM = 512
DSP_TM = 128
DSP_WIN = 256
CMB_T = 512
CMB_WIN = 256
CMB_MAX_SUB = (CMB_T + BF16_ROWS + CMB_WIN - 1) // CMB_WIN


def _router_kernel(x_ref, mod_ref, g_ref, wr_ref, h_ref, route_ref):
    h = _rms(x_ref[...], g_ref[...]) * (1.0 + mod_ref[4:5, :]) + mod_ref[3:4, :]
    hh, hl = _split2(h)
    h_ref[...] = hh
    a = _dot(hh, wr_ref[...])
    b = _dot(hl, wr_ref[...])
    ne = N_EXPERTS
    logits = a + b + pltpu.roll(a, LANES - ne, 1)
    lane = lax.broadcasted_iota(I32, logits.shape, 1)
    valid = lane < ne
    l1 = jnp.max(jnp.where(valid, logits, NEG_BIG), axis=-1, keepdims=True)
    e1 = jnp.min(jnp.where(valid & (logits == l1), lane, LANES), axis=-1, keepdims=True)
    rest = valid & (lane != e1)
    l2 = jnp.max(jnp.where(rest, logits, NEG_BIG), axis=-1, keepdims=True)
    e2 = jnp.min(jnp.where(rest & (logits == l2), lane, LANES), axis=-1, keepdims=True)
    w = jnp.exp(l2 - l1)
    g1 = 1.0 / (1.0 + w)
    g2 = w / (1.0 + w)
    lane8 = lax.broadcasted_iota(I32, route_ref.shape, 1)
    route_ref[...] = jnp.where(lane8 == 0, e1.astype(F32),
                               jnp.where(lane8 == 1, e2.astype(F32),
                                         jnp.where(lane8 == 2, g1, jnp.where(lane8 == 3, g2, 0.0))))


def _router(x2, mod_l, g, wr_packed, rows_per_batch):
    n, d = x2.shape
    tm = _pick(rows_per_batch, 512)
    bpb = rows_per_batch // tm
    return pl.pallas_call(
        _router_kernel,
        out_shape=(jax.ShapeDtypeStruct((n, d), BF16), jax.ShapeDtypeStruct((n, SUBLANES), F32)),
        grid=(n // tm,),
        in_specs=[
            pl.BlockSpec((tm, d), lambda i: (i, 0)),
            pl.BlockSpec((None, 6, d), lambda i: (i // bpb, 0, 0)),
            pl.BlockSpec((1, d), lambda i: (0, 0)),
            pl.BlockSpec((d, LANES), lambda i: (0, 0)),
        ],
        out_specs=(pl.BlockSpec((tm, d), lambda i: (i, 0)),
                   pl.BlockSpec((tm, SUBLANES), lambda i: (i, 0))),
        compiler_params=_params(("parallel",)),
        name="moe_router",
    )(x2, mod_l, g, wr_packed)


def _dispatch_kernel(w0_ref, nw_ref, h_hbm, pos_ref, gt_ref, xb_ref, gate_ref,
                     buf_ref, sem_ref, acc_ref, gacc_ref, cnt_ref):
    b = pl.program_id(0)
    nb = pl.num_programs(0)
    tm, win = DSP_TM, DSP_WIN

    @pl.when(b == 0)
    def _():
        cnt_ref[0] = 0

    g0 = cnt_ref[0]
    w0 = w0_ref[b]
    nw = nw_ref[b]
    p0 = b * tm
    b_next = jnp.minimum(b + 1, nb - 1)
    next_has = (b + 1 < nb) & (nw_ref[b_next] > 0)
    next_w0 = w0_ref[b_next]

    def copy(w, slot):
        return pltpu.make_async_copy(h_hbm.at[pl.ds(pl.multiple_of(w * win, win), win), :],
                                     buf_ref.at[slot], sem_ref.at[slot])

    acc_ref[...] = jnp.zeros(acc_ref.shape, F32)
    gacc_ref[...] = jnp.zeros(gacc_ref.shape, F32)

    @pl.when((b == 0) & (nw > 0))
    def _():
        copy(w0, 0).start()

    slot_id = p0 + lax.broadcasted_iota(I32, (tm, win), 0)

    def body(k, carry):
        slot = (g0 + k) % 2
        copy(w0 + k, slot).wait()

        @pl.when(k + 1 < nw)
        def _():
            copy(w0 + k + 1, 1 - slot).start()

        @pl.when((k + 1 == nw) & next_has)
        def _():
            copy(next_w0, 1 - slot).start()

        pos = pos_ref[w0 + k]
        gt = gt_ref[w0 + k]
        eq0 = pos[0:1, :] == slot_id
        eq1 = pos[1:2, :] == slot_id
        onehot = jnp.where(eq0 | eq1, 1.0, 0.0).astype(BF16)
        acc_ref[...] += _dot(onehot, buf_ref[slot])
        gsel = jnp.where(eq0, gt[0:1, :], 0.0) + jnp.where(eq1, gt[1:2, :], 0.0)
        gacc_ref[...] += jnp.sum(gsel, axis=-1, keepdims=True)
        return carry

    lax.fori_loop(0, nw, body, 0)

    @pl.when((nw == 0) & next_has)
    def _():
        copy(next_w0, g0 % 2).start()

    cnt_ref[0] = g0 + nw
    xb_ref[...] = acc_ref[...].astype(BF16)
    gate_ref[...] = gacc_ref[...]


def _dispatch(win0, nwin, h, pos_w, gate_w, n_steps):
    n, d = h.shape
    tm, win = DSP_TM, DSP_WIN
    nwt = n // win
    return pl.pallas_call(
        _dispatch_kernel,
        out_shape=(jax.ShapeDtypeStruct((n_steps * tm, d), BF16),
                   jax.ShapeDtypeStruct((n_steps * tm, 1), F32)),
        grid_spec=pltpu.PrefetchScalarGridSpec(
            num_scalar_prefetch=2,
            grid=(n_steps,),
            in_specs=[
                pl.BlockSpec(memory_space=pl.ANY),
                pl.BlockSpec((nwt, 2, win), lambda b, w0, nw: (0, 0, 0)),
                pl.BlockSpec((nwt, 2, win), lambda b, w0, nw: (0, 0, 0)),
            ],
            out_specs=(pl.BlockSpec((tm, d), lambda b, w0, nw: (b, 0)),
                       pl.BlockSpec((tm, 1), lambda b, w0, nw: (b, 0))),
            scratch_shapes=[
                pltpu.VMEM((2, win, d), BF16),
                pltpu.SemaphoreType.DMA((2,)),
                pltpu.VMEM((tm, d), F32),
                pltpu.VMEM((tm, 1), F32),
                pltpu.SMEM((1,), I32),
            ],
        ),
        compiler_params=_params(("arbitrary",)),
        name="moe_dispatch",
    )(win0, nwin, h, pos_w, gate_w)


def _expert_kernel(be_ref, used_ref, xb_ref, gate_ref, w1_ref, w3_ref, w2_ref, yb_ref, acc_ref):
    b = pl.program_id(0)
    j = pl.program_id(1)
    last = pl.num_programs(1) - 1
    live = b < used_ref[0]

    @pl.when(live)
    def _():
        xb = xb_ref[...]
        t = (_silu(_dot(xb, w1_ref[...])) * _dot(xb, w3_ref[...])).astype(BF16)
        part = _dot(t, w2_ref[...])

        @pl.when(j == 0)
        def _():
            acc_ref[...] = part

        @pl.when(j > 0)
        def _():
            acc_ref[...] += part

        @pl.when(j == last)
        def _():
            yb_ref[...] = (acc_ref[...] * gate_ref[...]).astype(BF16)

    @pl.when(jnp.logical_not(live) & (j == last))
    def _():
        yb_ref[...] = jnp.zeros(yb_ref.shape, BF16)


def _experts(block_expert, used, xb, gate_buf, w1, w3, w2, n_blocks):
    d = xb.shape[1]
    dff = w1.shape[2]
    tm = MOE_TM
    tf = dff // 2 if (dff // 2) % LANES == 0 else dff
    nj = dff // tf

    def jj(b, j, be, used):
        return jnp.where(b < used[0], j, nj - 1)

    return pl.pallas_call(
        _expert_kernel,
        out_shape=jax.ShapeDtypeStruct((n_blocks * tm, d), BF16),
        grid_spec=pltpu.PrefetchScalarGridSpec(
            num_scalar_prefetch=2,
            grid=(n_blocks, nj),
            in_specs=[
                pl.BlockSpec((tm, d), lambda b, j, be, used: (b, 0)),
                pl.BlockSpec((tm, 1), lambda b, j, be, used: (b, 0)),
                pl.BlockSpec((None, d, tf), lambda b, j, be, used: (be[b], 0, jj(b, j, be, used))),
                pl.BlockSpec((None, d, tf), lambda b, j, be, used: (be[b], 0, jj(b, j, be, used))),
                pl.BlockSpec((None, tf, d), lambda b, j, be, used: (be[b], jj(b, j, be, used), 0)),
            ],
            out_specs=pl.BlockSpec((tm, d), lambda b, j, be, used: (b, 0)),
            scratch_shapes=[pltpu.VMEM((tm, d), F32)],
        ),
        compiler_params=_params(("arbitrary", "arbitrary")),
        name="moe_experts",
    )(block_expert, used, xb, gate_buf, w1, w3, w2)


def _combine_kernel(ws_ref, nsub_ref, yb_hbm, pos_ref, x_ref, mod_ref, g_ref, o_ref,
                    buf_ref, sem_ref, xbuf_ref, xsem_ref, y_ref):
    i = pl.program_id(0)
    n_steps = pl.num_programs(0)
    ne = N_EXPERTS
    t, cw = CMB_T, CMB_WIN

    def start_row(step, e, sub):
        return pl.multiple_of(ws_ref[step * ne + e] + sub * cw, BF16_ROWS)

    def copy(step, e, slot):
        return pltpu.make_async_copy(yb_hbm.at[pl.ds(start_row(step, e, 0), cw), :],
                                     buf_ref.at[slot, e], sem_ref.at[slot, e])

    @pl.when(i == 0)
    def _():
        for e in range(ne):
            copy(0, e, 0).start()

    slot = i % 2

    @pl.when(i + 1 < n_steps)
    def _():
        for e in range(ne):
            copy(i + 1, e, 1 - slot).start()

    pe = pos_ref[...]
    pos = pe[:, 0:TOP_K]
    top = pe[:, TOP_K:2 * TOP_K]
    col = lax.broadcasted_iota(I32, (t, cw), 1)
    y_ref[...] = jnp.zeros(y_ref.shape, F32)

    def add_rows(rel, rows):
        onehot = jnp.where((rel[:, 0:1] == col) | (rel[:, 1:2] == col), 1.0, 0.0).astype(BF16)
        y_ref[...] += _dot(onehot, rows)

    for e in range(ne):
        copy(i, e, slot).wait()
        nsub = nsub_ref[i * ne + e]
        rel = jnp.where(top == e, pos - ws_ref[i * ne + e], -1)

        @pl.when(nsub > 0)
        def _():
            add_rows(rel, buf_ref[slot, e])

        for sub in range(1, CMB_MAX_SUB):
            @pl.when(nsub > sub)
            def _():
                extra = pltpu.make_async_copy(yb_hbm.at[pl.ds(start_row(i, e, sub), cw), :],
                                              xbuf_ref, xsem_ref)
                extra.start()
                extra.wait()
                add_rows(rel - sub * cw, xbuf_ref[...])

    o_ref[...] = x_ref[...] + mod_ref[5:6, :] * _rms(y_ref[...], g_ref[...])


def _combine(win_start, nsub, yb, pos, x2, mod_l, g, rows_per_batch):
    n, d = x2.shape
    t = CMB_T
    bpb = rows_per_batch // t
    return pl.pallas_call(
        _combine_kernel,
        out_shape=jax.ShapeDtypeStruct((n, d), F32),
        grid_spec=pltpu.PrefetchScalarGridSpec(
            num_scalar_prefetch=2,
            grid=(n // t,),
            in_specs=[
                pl.BlockSpec(memory_space=pl.ANY),
                pl.BlockSpec((t, 2 * TOP_K), lambda i, ws, ns: (i, 0)),
                pl.BlockSpec((t, d), lambda i, ws, ns: (i, 0)),
                pl.BlockSpec((None, 6, d), lambda i, ws, ns: (i // bpb, 0, 0)),
                pl.BlockSpec((1, d), lambda i, ws, ns: (0, 0)),
            ],
            out_specs=pl.BlockSpec((t, d), lambda i, ws, ns: (i, 0)),
            scratch_shapes=[
                pltpu.VMEM((2, N_EXPERTS, CMB_WIN, d), BF16),
                pltpu.SemaphoreType.DMA((2, N_EXPERTS)),
                pltpu.VMEM((CMB_WIN, d), BF16),
                pltpu.SemaphoreType.DMA(()),
                pltpu.VMEM((t, d), F32),
            ],
        ),
        compiler_params=_params(("arbitrary",)),
        name="moe_combine",
    )(win_start, nsub, yb, pos, x2, mod_l, g)


def _first_reaching(cum_t, targets):
    n = cum_t.shape[1]
    lo = jnp.zeros(targets.shape, I32)
    hi = jnp.full(targets.shape, n, I32)
    for _ in range(n.bit_length()):
        active = lo < hi
        mid = (lo + hi) // 2
        less = jnp.take_along_axis(cum_t, jnp.minimum(mid, n - 1), axis=1) < targets
        lo = jnp.where(active & less, mid + 1, lo)
        hi = jnp.where(active & jnp.logical_not(less), mid, hi)
    return lo


def _moe(x2, mod_l, gpre, gpost, w_router, w1, w3, w2, rows_per_batch):
    n, d = x2.shape
    ne, tm, dtm, win, t = N_EXPERTS, MOE_TM, DSP_TM, DSP_WIN, CMB_T
    wr_hi, wr_lo = _split2(w_router)
    wr_packed = jnp.zeros((d, LANES), BF16).at[:, 0:ne].set(wr_hi).at[:, ne:2 * ne].set(wr_lo)
    h, route = _router(x2, mod_l, gpre, wr_packed, rows_per_batch)

    top = route[:, 0:TOP_K].astype(I32)
    gates = route[:, TOP_K:2 * TOP_K]
    sel = jnp.sum((top[:, :, None] == jnp.arange(ne, dtype=I32)).astype(I32), axis=1)
    cum = jnp.cumsum(sel, axis=0)
    excl = cum - sel
    counts = cum[-1]
    padded = (counts + tm - 1) // tm * tm
    pend = jnp.cumsum(padded)
    pstart = pend - padded
    base = pstart[None, :] + excl
    pos = jnp.take_along_axis(base, top, axis=1)

    n_blocks = (n * TOP_K) // tm + ne + 1
    blk = jnp.arange(n_blocks, dtype=I32) * tm
    block_expert = jnp.minimum(jnp.sum((pend[None, :] <= blk[:, None]).astype(I32), axis=1), ne - 1)
    used = (pend[-1] // tm).astype(I32).reshape(1)

    n_steps = n_blocks * (tm // dtm)
    slot0 = jnp.arange(n_steps, dtype=I32) * dtm
    step_expert = jnp.repeat(block_expert, tm // dtm)
    rank0 = slot0 - pstart[step_expert]
    cnt_e = counts[step_expert]
    kmax = n // dtm + 1
    cum_t = cum.T
    first_of = _first_reaching(cum_t, jnp.broadcast_to(jnp.arange(kmax, dtype=I32) * dtm + 1, (ne, kmax)))
    last_of = _first_reaching(cum_t, counts[:, None])[:, 0]
    k0 = jnp.clip(rank0 // dtm, 0, kmax - 2)
    t_lo = first_of[step_expert, k0]
    t_hi = jnp.where(rank0 + dtm < cnt_e, first_of[step_expert, k0 + 1], last_of[step_expert])
    has = (slot0 < pend[-1]) & (rank0 < cnt_e)
    win0 = jnp.where(has, t_lo // win, 0).astype(I32)
    nwin = jnp.where(has, t_hi // win - t_lo // win + 1, 0).astype(I32)

    pos_w = pos.T.reshape(TOP_K, n // win, win).transpose(1, 0, 2)
    gate_w = gates.T.reshape(TOP_K, n // win, win).transpose(1, 0, 2)
    xb, gate_buf = _dispatch(win0, nwin, h, pos_w, gate_w, n_steps)
    yb = _experts(block_expert, used, xb, gate_buf, w1, w3, w2, n_blocks)

    base_b = base[::t]
    end_b = jnp.concatenate([base_b[1:], (pstart + counts)[None, :]], axis=0)
    win_start = base_b // BF16_ROWS * BF16_ROWS
    nsub = jnp.where(end_b > base_b, (end_b - win_start + CMB_WIN - 1) // CMB_WIN, 0)
    return _combine(win_start.astype(I32).reshape(-1), nsub.astype(I32).reshape(-1), yb,
                    jnp.concatenate([pos, top], axis=1), x2, mod_l, gpost, rows_per_batch)


def _reorder_w_in(w):
    main = jnp.concatenate([w[:, 0:3072], w[:, 3080:4616]], axis=1).astype(BF16)
    wdt = jnp.zeros((w.shape[0], LANES), BF16).at[:, 0:SSD_HEADS].set(w[:, 3072:3080].astype(BF16))
    return main, wdt


def _pad_row(v, width=LANES):
    return jnp.zeros((1, width), F32).at[0, 0:v.shape[0]].set(v)


def kernel(x, c, w_mod, b_mod, norm_mix_pre, norm_mix_post, norm_ffn_pre, norm_ffn_post, w_in, w_out, lambda_qk, attn_subln, ssd_conv_w, ssd_conv_b, ssd_dt_bias, ssd_a_log, ssd_d, ssd_norm, sconv_w, ffn_w1, ffn_w3, ffn_w2, moe_router, moe_w1, moe_w3, moe_w2):
    batch, seq, d = x.shape
    depth = w_mod.shape[0]
    n = batch * seq
    assert seq % SSD_CHUNK == 0 and w_in.shape[2] == PROJ_COLS + SSD_HEADS
    mod = _modulation(c, w_mod, b_mod).reshape(depth, batch, 6, d)
    x2 = x.reshape(n, d)
    for i in range(depth):
        mod_l = mod[i]
        w_main, w_dt = _reorder_w_in(w_in[i])
        proj, dt_raw = _inproj(x2, mod_l, norm_mix_pre[i][None, :], w_main, w_dt, seq)
        lam_init = 0.8 - 0.6 * math.exp(-0.3 * i)
        attn = _attention(proj, lambda_qk[i], attn_subln[i][None, :], batch, seq, lam_init)
        sc = _ssd_sconv(proj, dt_raw, ssd_conv_w[i], ssd_conv_b[i][None, :], _pad_row(ssd_dt_bias[i]),
                        _pad_row(ssd_a_log[i]), jnp.repeat(ssd_d[i], SSD_HEAD_DIM)[None, :],
                        ssd_norm[i][None, :], sconv_w[i], batch, seq)
        x2 = _outproj(attn, sc, w_out[i].astype(BF16), x2, mod_l, norm_mix_post[i][None, :], seq)
        if i % 2 == 0:
            x2 = _ffn(x2, mod_l, norm_ffn_pre[i][None, :], norm_ffn_post[i][None, :],
                      ffn_w1[i // 2].astype(BF16), ffn_w3[i // 2].astype(BF16), ffn_w2[i // 2].astype(BF16), seq)
        else:
            x2 = _moe(x2, mod_l, norm_ffn_pre[i][None, :], norm_ffn_post[i][None, :], moe_router[i // 2],
                      moe_w1[i // 2].astype(BF16), moe_w3[i // 2].astype(BF16), moe_w2[i // 2].astype(BF16), seq)
    return x2.reshape(batch, seq, d)
```

```python
import functools
import math

import jax
import jax.numpy as jnp
from jax import lax
from jax.experimental import pallas as pl
from jax.experimental.pallas import tpu as pltpu

F32 = jnp.float32
BF16 = jnp.bfloat16
I32 = jnp.int32

EPS = 1e-6
GROUP_WIDTH = 512
ATTN_HEADS = 4
ATTN_HEAD_DIM = 64
SSD_HEAD_DIM = 64
SSD_HEADS = 8
SSD_GROUPS = 2
SSD_STATE = 128
SSD_CONV = 4
SSD_CHUNK = 128
SSD_XBC = GROUP_WIDTH + 2 * SSD_GROUPS * SSD_STATE
SC_CONV = 3
N_EXPERTS = 8
TOP_K = 2

LANES = 128
SUBLANES = 8
BF16_ROWS = 16
VMEM_LIMIT = 48 * 1024 * 1024
NEG_BIG = -1e30

COL_Q, COL_K, COL_V, COL_Z, COL_XBC, COL_GB, COL_GC, COL_U = 0, 512, 1024, 1536, 2048, 3072, 3584, 4096
PROJ_COLS = 4608

NT_DIMS = (((1,), (1,)), ((), ()))


def _dot(a, b):
    return jnp.dot(a, b, preferred_element_type=F32)


def _dot_nt(a, b):
    return lax.dot_general(a, b, NT_DIMS, preferred_element_type=F32)


def _split2(x):
    hi = x.astype(BF16)
    lo = (x - hi.astype(F32)).astype(BF16)
    return hi, lo


def _split3(x):
    hi = x.astype(BF16)
    r = x - hi.astype(F32)
    mid = r.astype(BF16)
    lo = (r - mid.astype(F32)).astype(BF16)
    return hi, mid, lo


def _sigmoid(x):
    return 0.5 * jnp.tanh(0.5 * x) + 0.5


def _silu(x):
    return x * _sigmoid(x)


def _softplus(x):
    return jnp.maximum(x, 0.0) + jnp.log(1.0 + jnp.exp(-jnp.abs(x)))


def _rms(x, g):
    return x * lax.rsqrt(jnp.mean(x * x, axis=-1, keepdims=True) + EPS) * g


def _params(sem):
    return pltpu.CompilerParams(dimension_semantics=sem, vmem_limit_bytes=VMEM_LIMIT)


def _pick(n, pref):
    t = min(n, pref)
    assert n % t == 0, (n, pref)
    return t


def _mod_kernel(c_ref, w_ref, b_ref, o_ref):
    c = c_ref[...]
    ah, al = _split2(_silu(c))
    wh, wl = _split2(w_ref[...])
    o_ref[...] = _dot(ah, wh) + _dot(ah, wl) + _dot(al, wh) + b_ref[...]


def _modulation(c, w_mod, b_mod):
    depth, d, n6 = w_mod.shape
    b = c.shape[0]
    tn = _pick(n6, 1536)
    return pl.pallas_call(
        _mod_kernel,
        out_shape=jax.ShapeDtypeStruct((depth, b, n6), F32),
        grid=(depth, n6 // tn),
        in_specs=[
            pl.BlockSpec((b, d), lambda l, j: (0, 0)),
            pl.BlockSpec((None, d, tn), lambda l, j: (l, 0, j)),
            pl.BlockSpec((None, 1, tn), lambda l, j: (l, 0, j)),
        ],
        out_specs=pl.BlockSpec((None, b, tn), lambda l, j: (l, 0, j)),
        compiler_params=_params(("parallel", "parallel")),
        name="modulation",
    )(c, w_mod, b_mod.reshape(depth, 1, n6))


def _inproj_kernel(x_ref, mod_ref, g_ref, w_ref, wdt_ref, o_ref, dt_ref, h_ref):
    @pl.when(pl.program_id(1) == 0)
    def _():
        h = _rms(x_ref[...], g_ref[...]) * (1.0 + mod_ref[1:2, :]) + mod_ref[0:1, :]
        hb = h.astype(BF16)
        h_ref[...] = hb
        dt_ref[...] = _dot(hb, wdt_ref[...])

    o_ref[...] = _dot(h_ref[...], w_ref[...]).astype(BF16)


def _inproj(x2, mod_l, g, w, wdt, rows_per_batch):
    n, d = x2.shape
    tm = _pick(rows_per_batch, 1024)
    tn = 1536
    bpb = rows_per_batch // tm
    return pl.pallas_call(
        _inproj_kernel,
        out_shape=(jax.ShapeDtypeStruct((n, PROJ_COLS), BF16),
                   jax.ShapeDtypeStruct((n, LANES), F32)),
        grid=(n // tm, PROJ_COLS // tn),
        in_specs=[
            pl.BlockSpec((tm, d), lambda i, j: (i, 0)),
            pl.BlockSpec((None, 6, d), lambda i, j: (i // bpb, 0, 0)),
            pl.BlockSpec((1, d), lambda i, j: (0, 0)),
            pl.BlockSpec((d, tn), lambda i, j: (0, j)),
            pl.BlockSpec((d, LANES), lambda i, j: (0, 0)),
        ],
        out_specs=(pl.BlockSpec((tm, tn), lambda i, j: (i, j)),
                   pl.BlockSpec((tm, LANES), lambda i, j: (i, 0))),
        scratch_shapes=[pltpu.VMEM((tm, d), BF16)],
        compiler_params=_params(("parallel", "arbitrary")),
        name="inproj",
    )(x2, mod_l, g, w, wdt)


def _attn_kernel(q_ref, k_ref, v_ref, lq_ref, sub_ref, o_ref, qt_ref, vt_ref, m_ref, l_ref, acc_ref,
                 s_ref, cmax_ref, *, tq, lam_init):
    i = pl.program_id(2)
    d = ATTN_HEAD_DIM
    n_chunks = vt_ref.shape[0]

    @pl.when(i == 0)
    def _():
        for cidx in range(n_chunks):
            vt_ref[cidx] = v_ref[cidx * tq:(cidx + 1) * tq, :].astype(F32).T.astype(BF16)

    qt = (q_ref[...].astype(F32) * (d ** -0.5 * math.log2(math.e))).T
    row = lax.broadcasted_iota(I32, qt.shape, 0)
    qt_ref[:, 0:tq] = jnp.where(row < d, qt, 0.0).astype(BF16)
    qt_ref[:, tq:] = jnp.where(row >= d, qt, 0.0).astype(BF16)
    m_ref[...] = jnp.full(m_ref.shape, NEG_BIG, F32)
    l_ref[...] = jnp.zeros(l_ref.shape, F32)
    acc_ref[...] = jnp.zeros(acc_ref.shape, F32)

    def scores(j, slot, masked):
        start = pl.multiple_of(j * tq, tq)
        s = _dot(k_ref[pl.ds(start, tq), :], qt_ref[...])
        if masked:
            kk = lax.broadcasted_iota(I32, (tq, tq), 0)
            qq = lax.broadcasted_iota(I32, (tq, tq), 1)
            keep = kk <= qq
            s = jnp.where(jnp.concatenate([keep, keep], axis=1), s, NEG_BIG)
        s_ref[slot] = s
        cmax_ref[slot] = jnp.max(s, axis=0, keepdims=True)

    def accumulate(j, slot):
        m_prev = m_ref[...]
        m_new = jnp.maximum(m_prev, cmax_ref[slot])
        alpha = jnp.exp2(m_prev - m_new)
        p = jnp.exp2(s_ref[slot] - m_new)
        l_ref[...] = alpha * l_ref[...] + jnp.sum(p, axis=0, keepdims=True)
        acc_ref[...] = alpha * acc_ref[...] + _dot(vt_ref[j], p.astype(BF16))
        m_ref[...] = m_new

    scores(i, 0, True)

    def step(t, slot):
        scores(t, 1 - slot, False)
        accumulate(jnp.where(t == 0, i, t - 1), slot)

    def body(u, carry):
        step(2 * u, 0)
        step(2 * u + 1, 1)
        return carry

    lax.fori_loop(0, i // 2, body, 0)

    @pl.when(i % 2 == 1)
    def _():
        step(i - 1, 0)
        accumulate(i - 1, 1)

    @pl.when(i % 2 == 0)
    def _():
        accumulate(jnp.where(i == 0, i, i - 1), 0)

    lq = lq_ref[...]
    lam = (jnp.exp(jnp.sum(lq[0:1, :] * lq[1:2, :], axis=-1, keepdims=True))
           - jnp.exp(jnp.sum(lq[2:3, :] * lq[3:4, :], axis=-1, keepdims=True)) + lam_init)
    inv = 1.0 / l_ref[...]
    ot = acc_ref[:, 0:tq] * inv[:, 0:tq] - lam * (acc_ref[:, tq:] * inv[:, tq:])
    ot = ot * lax.rsqrt(jnp.mean(ot * ot, axis=0, keepdims=True) + EPS)
    o_ref[...] = (ot.T * sub_ref[...] * (1.0 - lam_init)).astype(BF16)


def _attention(proj, lambda_qk, subln, batch, seq, lam_init):
    n = proj.shape[0]
    tq = _pick(seq, 512)
    nq = seq // tq
    h = ATTN_HEADS
    return pl.pallas_call(
        functools.partial(_attn_kernel, tq=tq, lam_init=lam_init),
        out_shape=jax.ShapeDtypeStruct((n, GROUP_WIDTH), BF16),
        grid=(batch, h, nq),
        in_specs=[
            pl.BlockSpec((tq, LANES), lambda b, hh, i: (b * nq + i, COL_Q // LANES + hh)),
            pl.BlockSpec((seq, LANES), lambda b, hh, i: (b, COL_K // LANES + hh)),
            pl.BlockSpec((seq, LANES), lambda b, hh, i: (b, COL_V // LANES + hh)),
            pl.BlockSpec((4, ATTN_HEAD_DIM), lambda b, hh, i: (0, 0)),
            pl.BlockSpec((1, LANES), lambda b, hh, i: (0, 0)),
        ],
        out_specs=pl.BlockSpec((tq, LANES), lambda b, hh, i: (b * nq + i, hh)),
        scratch_shapes=[
            pltpu.VMEM((LANES, 2 * tq), BF16),
            pltpu.VMEM((nq, LANES, tq), BF16),
            pltpu.VMEM((1, 2 * tq), F32),
            pltpu.VMEM((1, 2 * tq), F32),
            pltpu.VMEM((LANES, 2 * tq), F32),
            pltpu.VMEM((2, tq, 2 * tq), F32),
            pltpu.VMEM((2, 1, 2 * tq), F32),
        ],
        compiler_params=_params(("parallel", "parallel", "arbitrary")),
        name="diff_attention",
    )(proj, proj, proj, lambda_qk, subln)


def _causal_conv(x, carry, w, width):
    row = lax.broadcasted_iota(I32, carry.shape, 0)
    out = x * w[width - 1:width, :]
    for k in range(1, width):
        xr = pltpu.roll(x, k, 0)
        cr = pltpu.roll(carry, k, 0)
        head = jnp.where(row < k, cr, xr[0:SUBLANES, :])
        xk = jnp.concatenate([head, xr[SUBLANES:, :]], axis=0)
        out = out + xk * w[width - 1 - k:width - k, :]
    return out


def _ssd_kernel(z_ref, xbc_ref, gb_ref, gc_ref, u_ref, dt_ref, cw_ref, cb_ref, dtb_ref, alog_ref,
                dskip_ref, ng_ref, sw_ref, o_ref,
                cx_ref, cs_ref, state_ref, xs_ref, bm_ref, cm_ref, *, tc):
    L = SSD_CHUNK
    gw = GROUP_WIDTH
    ns = SSD_STATE

    @pl.when(pl.program_id(1) == 0)
    def _():
        cx_ref[...] = jnp.zeros(cx_ref.shape, F32)
        cs_ref[...] = jnp.zeros(cs_ref.shape, F32)
        state_ref[...] = jnp.zeros(state_ref.shape, F32)

    pu = gc_ref[...].astype(F32) * u_ref[...].astype(F32)
    sconv = gb_ref[...].astype(F32) * _causal_conv(pu, cs_ref[...], sw_ref[...], SC_CONV)
    cs_ref[...] = pu[tc - SUBLANES:, :]
    o_ref[:, gw:] = sconv.astype(BF16)

    xbc = xbc_ref[...].astype(F32)
    act = _silu(_causal_conv(xbc, cx_ref[...], cw_ref[...], SSD_CONV) + cb_ref[...])
    cx_ref[...] = xbc[tc - SUBLANES:, :]
    xs_ref[...] = act[:, 0:gw]
    bm_ref[...] = act[:, gw:gw + SSD_GROUPS * ns]
    cm_ref[...] = act[:, gw + SSD_GROUPS * ns:]

    lane = lax.broadcasted_iota(I32, (L, LANES), 1)
    lo = lane < SSD_HEAD_DIM
    rr = lax.broadcasted_iota(I32, (L, L), 0)
    cc = lax.broadcasted_iota(I32, (L, L), 1)
    tril = cc <= rr
    ltri = jnp.where(tril, 1.0, 0.0).astype(BF16)
    head_lane = lane < SSD_HEADS
    a_row = -jnp.exp(alog_ref[...])

    def pair_pattern(mat, h0):
        lo_b = lo[0:mat.shape[0], :]
        return jnp.where(lo_b, mat[:, h0:h0 + 1], mat[:, h0 + 1:h0 + 2])

    def chunk_body(c, carry):
        r0 = pl.multiple_of(c * L, L)
        rows = pl.ds(r0, L)
        dt = _softplus(dt_ref[rows, :] + dtb_ref[...])
        da = jnp.where(head_lane, dt * a_row, 0.0)
        d_hi, d_mid, d_lo = _split3(da)
        acum = _dot(ltri, d_hi) + _dot(ltri, d_mid) + _dot(ltri, d_lo)
        acum_t = acum.T
        a_last = acum[L - 1:L, :]
        ys = []
        for g in range(SSD_GROUPS):
            bg = bm_ref[rows, g * ns:(g + 1) * ns]
            cg = cm_ref[rows, g * ns:(g + 1) * ns].astype(BF16)
            cb = _dot_nt(cg, bg.astype(BF16))
            bg_t = bg.T.astype(BF16)
            for jp in range(SSD_HEADS // SSD_GROUPS // 2):
                j = g * (SSD_HEADS // SSD_GROUPS // 2) + jp
                h0 = 2 * j
                xs = xs_ref[rows, j * LANES:(j + 1) * LANES]
                xdt = xs * pair_pattern(dt, h0)
                y = None
                for hh, keep in ((h0, lo), (h0 + 1, jnp.logical_not(lo))):
                    seg = acum[:, hh:hh + 1] - acum_t[hh:hh + 1, :]
                    dec = jnp.exp(jnp.where(tril, seg, NEG_BIG))
                    mm = (dec * cb).astype(BF16)
                    part = _dot(mm, jnp.where(keep, xdt, 0.0).astype(BF16))
                    y = part if y is None else y + part
                st = state_ref[j]
                y = y + _dot(cg, st.astype(BF16)) * jnp.exp(pair_pattern(acum, h0))
                dte = jnp.exp(pair_pattern(a_last - acum, h0))
                contrib = _dot(bg_t, (xdt * dte).astype(BF16))
                state_ref[j] = st * jnp.exp(pair_pattern(a_last, h0)) + contrib
                ys.append(y + dskip_ref[:, j * LANES:(j + 1) * LANES] * xs)
        yv = jnp.concatenate(ys, axis=-1)
        z = z_ref[rows, :].astype(F32)
        vv = yv * _silu(z)
        gwid = gw // SSD_GROUPS
        outs = []
        for g in range(SSD_GROUPS):
            vg = vv[:, g * gwid:(g + 1) * gwid]
            outs.append(vg * lax.rsqrt(jnp.mean(vg * vg, axis=-1, keepdims=True) + EPS))
        o_ref[rows, 0:gw] = (jnp.concatenate(outs, axis=-1) * ng_ref[...]).astype(BF16)
        return carry

    lax.fori_loop(0, tc // L, chunk_body, 0)


def _ssd_sconv(proj, dt_raw, cw, cb, dtb, alog, dskip, ng, sw, batch, seq):
    n = proj.shape[0]
    tc = _pick(seq, 512)
    nt = seq // tc
    row = lambda b, t: b * nt + t
    const = lambda b, t: (0, 0)
    return pl.pallas_call(
        functools.partial(_ssd_kernel, tc=tc),
        out_shape=jax.ShapeDtypeStruct((n, 2 * GROUP_WIDTH), BF16),
        grid=(batch, nt),
        in_specs=[
            pl.BlockSpec((tc, GROUP_WIDTH), lambda b, t: (row(b, t), COL_Z // GROUP_WIDTH)),
            pl.BlockSpec((tc, SSD_XBC), lambda b, t: (row(b, t), COL_XBC // SSD_XBC)),
            pl.BlockSpec((tc, GROUP_WIDTH), lambda b, t: (row(b, t), COL_GB // GROUP_WIDTH)),
            pl.BlockSpec((tc, GROUP_WIDTH), lambda b, t: (row(b, t), COL_GC // GROUP_WIDTH)),
            pl.BlockSpec((tc, GROUP_WIDTH), lambda b, t: (row(b, t), COL_U // GROUP_WIDTH)),
            pl.BlockSpec((tc, LANES), lambda b, t: (row(b, t), 0)),
            pl.BlockSpec((SSD_CONV, SSD_XBC), const),
            pl.BlockSpec((1, SSD_XBC), const),
            pl.BlockSpec((1, LANES), const),
            pl.BlockSpec((1, LANES), const),
            pl.BlockSpec((1, GROUP_WIDTH), const),
            pl.BlockSpec((1, GROUP_WIDTH), const),
            pl.BlockSpec((SC_CONV, GROUP_WIDTH), const),
        ],
        out_specs=pl.BlockSpec((tc, 2 * GROUP_WIDTH), lambda b, t: (row(b, t), 0)),
        scratch_shapes=[
            pltpu.VMEM((SUBLANES, SSD_XBC), F32),
            pltpu.VMEM((SUBLANES, GROUP_WIDTH), F32),
            pltpu.VMEM((SSD_HEADS // 2, SSD_STATE, LANES), F32),
            pltpu.VMEM((tc, GROUP_WIDTH), F32),
            pltpu.VMEM((tc, SSD_GROUPS * SSD_STATE), F32),
            pltpu.VMEM((tc, SSD_GROUPS * SSD_STATE), F32),
        ],
        compiler_params=_params(("parallel", "arbitrary")),
        name="ssd_sconv",
    )(proj, proj, proj, proj, proj, dt_raw, cw, cb, dtb, alog, dskip, ng, sw)


def _outproj_kernel(a_ref, sc_ref, w_ref, x_ref, mod_ref, g_ref, o_ref):
    gw = GROUP_WIDTH
    y = _dot(a_ref[...], w_ref[0:gw, :]) + _dot(sc_ref[...], w_ref[gw:, :])
    o_ref[...] = x_ref[...] + mod_ref[2:3, :] * _rms(y, g_ref[...])


def _outproj(attn, sc, w, x2, mod_l, g, rows_per_batch):
    n, d = x2.shape
    tm = _pick(rows_per_batch, 512)
    bpb = rows_per_batch // tm
    return pl.pallas_call(
        _outproj_kernel,
        out_shape=jax.ShapeDtypeStruct((n, d), F32),
        grid=(n // tm,),
        in_specs=[
            pl.BlockSpec((tm, GROUP_WIDTH), lambda i: (i, 0)),
            pl.BlockSpec((tm, 2 * GROUP_WIDTH), lambda i: (i, 0)),
            pl.BlockSpec((3 * GROUP_WIDTH, d), lambda i: (0, 0)),
            pl.BlockSpec((tm, d), lambda i: (i, 0)),
            pl.BlockSpec((None, 6, d), lambda i: (i // bpb, 0, 0)),
            pl.BlockSpec((1, d), lambda i: (0, 0)),
        ],
        out_specs=pl.BlockSpec((tm, d), lambda i: (i, 0)),
        compiler_params=_params(("parallel",)),
        name="outproj",
    )(attn, sc, w, x2, mod_l, g)


def _ffn_kernel(x_ref, mod_ref, gpre_ref, gpost_ref, w1_ref, w3_ref, w2_ref, o_ref, h_ref, acc_ref):
    j = pl.program_id(1)

    @pl.when(j == 0)
    def _():
        h = _rms(x_ref[...], gpre_ref[...]) * (1.0 + mod_ref[4:5, :]) + mod_ref[3:4, :]
        h_ref[...] = h.astype(BF16)

    h = h_ref[...]
    t = (_silu(_dot(h, w1_ref[...])) * _dot(h, w3_ref[...])).astype(BF16)
    part = _dot(t, w2_ref[...])

    @pl.when(j == 0)
    def _():
        acc_ref[...] = part

    @pl.when(j > 0)
    def _():
        acc_ref[...] += part

    @pl.when(j == pl.num_programs(1) - 1)
    def _():
        o_ref[...] = x_ref[...] + mod_ref[5:6, :] * _rms(acc_ref[...], gpost_ref[...])


def _ffn(x2, mod_l, gpre, gpost, w1, w3, w2, rows_per_batch):
    n, d = x2.shape
    dff = w1.shape[1]
    tm = _pick(rows_per_batch, 512)
    tf = dff // 2 if (dff // 2) % LANES == 0 else dff
    bpb = rows_per_batch // tm
    return pl.pallas_call(
        _ffn_kernel,
        out_shape=jax.ShapeDtypeStruct((n, d), F32),
        grid=(n // tm, dff // tf),
        in_specs=[
            pl.BlockSpec((tm, d), lambda i, j: (i, 0)),
            pl.BlockSpec((None, 6, d), lambda i, j: (i // bpb, 0, 0)),
            pl.BlockSpec((1, d), lambda i, j: (0, 0)),
            pl.BlockSpec((1, d), lambda i, j: (0, 0)),
            pl.BlockSpec((d, tf), lambda i, j: (0, j)),
            pl.BlockSpec((d, tf), lambda i, j: (0, j)),
            pl.BlockSpec((tf, d), lambda i, j: (j, 0)),
        ],
        out_specs=pl.BlockSpec((tm, d), lambda i, j: (i, 0)),
        scratch_shapes=[pltpu.VMEM((tm, d), BF16), pltpu.VMEM((tm, d), F32)],
        compiler_params=_params(("parallel", "arbitrary")),
        name="dense_ffn",
    )(x2, mod_l, gpre, gpost, w1, w3, w2)


MOE_TM = 512
DSP_TM = 128
DSP_WIN = 256
DSP_DEPTH = 4
CMB_T = 512
CMB_WIN = 256
CMB_MAX_SUB = (CMB_T + BF16_ROWS + CMB_WIN - 1) // CMB_WIN


def _router_kernel(x_ref, mod_ref, g_ref, wr_ref, h_ref, route_ref, wcum_ref, run_ref):
    @pl.when(pl.program_id(0) == 0)
    def _():
        run_ref[...] = jnp.zeros(run_ref.shape, F32)

    h = _rms(x_ref[...], g_ref[...]) * (1.0 + mod_ref[4:5, :]) + mod_ref[3:4, :]
    hh, hl = _split2(h)
    h_ref[...] = hh
    a = _dot(hh, wr_ref[...])
    b = _dot(hl, wr_ref[...])
    ne = N_EXPERTS
    logits = a + b + pltpu.roll(a, LANES - ne, 1)
    lane = lax.broadcasted_iota(I32, logits.shape, 1)
    valid = lane < ne
    l1 = jnp.max(jnp.where(valid, logits, NEG_BIG), axis=-1, keepdims=True)
    e1 = jnp.min(jnp.where(valid & (logits == l1), lane, LANES), axis=-1, keepdims=True)
    rest = valid & (lane != e1)
    l2 = jnp.max(jnp.where(rest, logits, NEG_BIG), axis=-1, keepdims=True)
    e2 = jnp.min(jnp.where(rest & (logits == l2), lane, LANES), axis=-1, keepdims=True)
    w = jnp.exp(l2 - l1)
    g1 = 1.0 / (1.0 + w)
    g2 = w / (1.0 + w)

    tm = logits.shape[0]
    sel = jnp.where((lane == e1) | (lane == e2), 1.0, 0.0)
    rr = lax.broadcasted_iota(I32, (tm, tm), 0)
    cc = lax.broadcasted_iota(I32, (tm, tm), 1)
    before = jnp.where(cc < rr, 1.0, 0.0).astype(BF16)
    excl = _dot(before, sel.astype(BF16)) + run_ref[...]
    rank1 = jnp.sum(jnp.where(lane == e1, excl, 0.0), axis=-1, keepdims=True)
    rank2 = jnp.sum(jnp.where(lane == e2, excl, 0.0), axis=-1, keepdims=True)
    total = excl + sel
    for wdx in range(tm // DSP_WIN):
        wcum_ref[wdx:wdx + 1, :] = total[(wdx + 1) * DSP_WIN - 1:(wdx + 1) * DSP_WIN, :]
    run_ref[...] = total[tm - 1:tm, :]

    lane8 = lax.broadcasted_iota(I32, route_ref.shape, 1)
    cols = (e1.astype(F32), e2.astype(F32), g1, g2, rank1, rank2)
    out = jnp.zeros(route_ref.shape, F32)
    for idx, col in enumerate(cols):
        out = jnp.where(lane8 == idx, col, out)
    route_ref[...] = out


def _router(x2, mod_l, g, wr_packed, rows_per_batch):
    n, d = x2.shape
    tm = _pick(rows_per_batch, 512)
    assert tm % DSP_WIN == 0
    bpb = rows_per_batch // tm
    return pl.pallas_call(
        _router_kernel,
        out_shape=(jax.ShapeDtypeStruct((n, d), BF16), jax.ShapeDtypeStruct((n, SUBLANES), F32),
                   jax.ShapeDtypeStruct((n // tm, tm // DSP_WIN, LANES), F32)),
        grid=(n // tm,),
        in_specs=[
            pl.BlockSpec((tm, d), lambda i: (i, 0)),
            pl.BlockSpec((None, 6, d), lambda i: (i // bpb, 0, 0)),
            pl.BlockSpec((1, d), lambda i: (0, 0)),
            pl.BlockSpec((d, LANES), lambda i: (0, 0)),
        ],
        out_specs=(pl.BlockSpec((tm, d), lambda i: (i, 0)),
                   pl.BlockSpec((tm, SUBLANES), lambda i: (i, 0)),
                   pl.BlockSpec((None, tm // DSP_WIN, LANES), lambda i: (i, 0, 0))),
        scratch_shapes=[pltpu.VMEM((1, LANES), F32)],
        compiler_params=_params(("arbitrary",)),
        name="moe_router",
    )(x2, mod_l, g, wr_packed)


def _dispatch_kernel(meta_ref, wc_ref, h_hbm, pos_ref, gt_ref, xb_ref, gate_ref,
                     buf_ref, sem_ref, acc_ref, gacc_ref, st_ref):
    b = pl.program_id(0)
    nb = pl.num_programs(0)
    ne = N_EXPERTS
    tm, win = DSP_TM, DSP_WIN
    nslot = DSP_DEPTH + 1
    nwt = pos_ref.shape[0]

    def copy(w, slot):
        return pltpu.make_async_copy(h_hbm.at[pl.ds(pl.multiple_of(w * win, win), win), :],
                                     buf_ref.at[slot], sem_ref.at[slot])

    def step_range(s):
        p = s * tm
        e = jnp.zeros((), I32)
        for j in range(ne - 1):
            e = e + (p >= meta_ref[2 * ne + j]).astype(I32)
        r0 = p - meta_ref[e]
        cnt = meta_ref[ne + e]
        r1 = jnp.minimum(r0 + tm, cnt)

        def first_window_reaching(target):
            def halve(_, c):
                lo, hi = c
                mid = (lo + hi) // 2
                less = wc_ref[e * nwt + mid] < target
                return jnp.where(less, mid + 1, lo), jnp.where(less, hi, mid)

            return lax.fori_loop(0, max(nwt - 1, 1).bit_length(), halve,
                                 (jnp.zeros((), I32), jnp.full((), nwt - 1, I32)))[0]

        w_first = first_window_reaching(r0 + 1)
        w_last = first_window_reaching(r1)
        has = r0 < cnt
        return jnp.where(has, w_first, 0), jnp.where(has, w_last - w_first + 1, 0)

    def produce():
        def exhausted(c):
            return (c[0] < nb) & (c[1] >= c[3])

        def next_step(c):
            s = c[0] + 1
            w_first, count = step_range(jnp.minimum(s, nb - 1))
            return s, jnp.zeros_like(c[1]), w_first, count

        ps, pk, pw, pn = lax.while_loop(exhausted, next_step,
                                        (st_ref[0], st_ref[1], st_ref[4], st_ref[5]))
        st_ref[0] = ps
        st_ref[4] = pw
        st_ref[5] = pn

        @pl.when(ps < nb)
        def _():
            issued = st_ref[2]
            copy(pw + pk, issued % nslot).start()
            st_ref[1] = pk + 1
            st_ref[2] = issued + 1

        @pl.when(ps >= nb)
        def _():
            st_ref[1] = pk

    w0, nw = step_range(b)

    @pl.when(b == 0)
    def _():
        for idx in range(4):
            st_ref[idx] = 0
        st_ref[4] = w0
        st_ref[5] = nw
        for _ in range(DSP_DEPTH):
            produce()

    g0 = st_ref[3]
    acc_ref[...] = jnp.zeros(acc_ref.shape, F32)
    gacc_ref[...] = jnp.zeros(gacc_ref.shape, F32)
    slot_id = b * tm + lax.broadcasted_iota(I32, (tm, win), 0)

    def body(k, carry):
        slot = (g0 + k) % nslot
        copy(w0 + k, slot).wait()
        produce()
        pos = pos_ref[w0 + k]
        gt = gt_ref[w0 + k]
        eq0 = pos[0:1, :] == slot_id
        eq1 = pos[1:2, :] == slot_id
        onehot = jnp.where(eq0 | eq1, 1.0, 0.0).astype(BF16)
        acc_ref[...] += _dot(onehot, buf_ref[slot])
        gsel = jnp.where(eq0, gt[0:1, :], 0.0) + jnp.where(eq1, gt[1:2, :], 0.0)
        gacc_ref[...] += jnp.sum(gsel, axis=-1, keepdims=True)
        return carry

    lax.fori_loop(0, nw, body, 0)
    st_ref[3] = g0 + nw
    xb_ref[...] = acc_ref[...].astype(BF16)
    gate_ref[...] = gacc_ref[...]


def _dispatch(meta, wcum_flat, h, pos_w, gate_w, n_steps):
    n, d = h.shape
    tm, win = DSP_TM, DSP_WIN
    nwt = n // win
    return pl.pallas_call(
        _dispatch_kernel,
        out_shape=(jax.ShapeDtypeStruct((n_steps * tm, d), BF16),
                   jax.ShapeDtypeStruct((n_steps * tm, 1), F32)),
        grid_spec=pltpu.PrefetchScalarGridSpec(
            num_scalar_prefetch=2,
            grid=(n_steps,),
            in_specs=[
                pl.BlockSpec(memory_space=pl.ANY),
                pl.BlockSpec((nwt, 2, win), lambda b, w0, nw: (0, 0, 0)),
                pl.BlockSpec((nwt, 2, win), lambda b, w0, nw: (0, 0, 0)),
            ],
            out_specs=(pl.BlockSpec((tm, d), lambda b, w0, nw: (b, 0)),
                       pl.BlockSpec((tm, 1), lambda b, w0, nw: (b, 0))),
            scratch_shapes=[
                pltpu.VMEM((DSP_DEPTH + 1, win, d), BF16),
                pltpu.SemaphoreType.DMA((DSP_DEPTH + 1,)),
                pltpu.VMEM((tm, d), F32),
                pltpu.VMEM((tm, 1), F32),
                pltpu.SMEM((6,), I32),
            ],
        ),
        compiler_params=_params(("arbitrary",)),
        name="moe_dispatch",
    )(meta, wcum_flat, h, pos_w, gate_w)


def _expert_kernel(be_ref, used_ref, xb_ref, gate_ref, w1_ref, w3_ref, w2_ref, yb_ref, acc_ref):
    b = pl.program_id(0)
    j = pl.program_id(1)
    last = pl.num_programs(1) - 1
    live = b < used_ref[0]

    @pl.when(live)
    def _():
        xb = xb_ref[...]
        t = (_silu(_dot(xb, w1_ref[...])) * _dot(xb, w3_ref[...])).astype(BF16)
        part = _dot(t, w2_ref[...])

        @pl.when(j == 0)
        def _():
            acc_ref[...] = part

        @pl.when(j > 0)
        def _():
            acc_ref[...] += part

        @pl.when(j == last)
        def _():
            yb_ref[...] = (acc_ref[...] * gate_ref[...]).astype(BF16)

    @pl.when(jnp.logical_not(live) & (j == last))
    def _():
        yb_ref[...] = jnp.zeros(yb_ref.shape, BF16)


def _experts(block_expert, used, xb, gate_buf, w1, w3, w2, n_blocks):
    d = xb.shape[1]
    dff = w1.shape[2]
    tm = MOE_TM
    tf = dff // 2 if (dff // 2) % LANES == 0 else dff
    nj = dff // tf

    def jj(b, j, be, used):
        return jnp.where(b < used[0], j, nj - 1)

    return pl.pallas_call(
        _expert_kernel,
        out_shape=jax.ShapeDtypeStruct((n_blocks * tm, d), BF16),
        grid_spec=pltpu.PrefetchScalarGridSpec(
            num_scalar_prefetch=2,
            grid=(n_blocks, nj),
            in_specs=[
                pl.BlockSpec((tm, d), lambda b, j, be, used: (b, 0)),
                pl.BlockSpec((tm, 1), lambda b, j, be, used: (b, 0)),
                pl.BlockSpec((None, d, tf), lambda b, j, be, used: (be[b], 0, jj(b, j, be, used))),
                pl.BlockSpec((None, d, tf), lambda b, j, be, used: (be[b], 0, jj(b, j, be, used))),
                pl.BlockSpec((None, tf, d), lambda b, j, be, used: (be[b], jj(b, j, be, used), 0)),
            ],
            out_specs=pl.BlockSpec((tm, d), lambda b, j, be, used: (b, 0)),
            scratch_shapes=[pltpu.VMEM((tm, d), F32)],
        ),
        compiler_params=_params(("arbitrary", "arbitrary")),
        name="moe_experts",
    )(block_expert, used, xb, gate_buf, w1, w3, w2)


def _combine_kernel(ws_ref, nsub_ref, yb_hbm, pos_ref, x_ref, mod_ref, g_ref, o_ref,
                    buf_ref, sem_ref, xbuf_ref, xsem_ref, y_ref):
    i = pl.program_id(0)
    n_steps = pl.num_programs(0)
    ne = N_EXPERTS
    t, cw = CMB_T, CMB_WIN

    def start_row(step, e, sub):
        return pl.multiple_of(ws_ref[step * ne + e] + sub * cw, BF16_ROWS)

    def copy(step, e, slot):
        return pltpu.make_async_copy(yb_hbm.at[pl.ds(start_row(step, e, 0), cw), :],
                                     buf_ref.at[slot, e], sem_ref.at[slot, e])

    @pl.when(i == 0)
    def _():
        for e in range(ne):
            copy(0, e, 0).start()

    slot = i % 2

    @pl.when(i + 1 < n_steps)
    def _():
        for e in range(ne):
            copy(i + 1, e, 1 - slot).start()

    pe = pos_ref[...]
    pos = pe[:, 0:TOP_K]
    top = pe[:, TOP_K:2 * TOP_K]
    col = lax.broadcasted_iota(I32, (t, cw), 1)

    def onehot(rel):
        return jnp.where((rel[:, 0:1] == col) | (rel[:, 1:2] == col), 1.0, 0.0).astype(BF16)

    rels = []
    for e in range(ne):
        copy(i, e, slot).wait()
        rels.append(jnp.where(top == e, pos - ws_ref[i * ne + e], -1))
    picks = jnp.concatenate([onehot(rel) for rel in rels], axis=1)
    y_ref[...] = _dot(picks, buf_ref[slot].reshape(ne * cw, buf_ref.shape[-1]))

    for e in range(ne):
        nsub = nsub_ref[i * ne + e]
        for sub in range(1, CMB_MAX_SUB):
            @pl.when(nsub > sub)
            def _():
                extra = pltpu.make_async_copy(yb_hbm.at[pl.ds(start_row(i, e, sub), cw), :],
                                              xbuf_ref, xsem_ref)
                extra.start()
                extra.wait()
                y_ref[...] += _dot(onehot(rels[e] - sub * cw), xbuf_ref[...])

    o_ref[...] = x_ref[...] + mod_ref[5:6, :] * _rms(y_ref[...], g_ref[...])


def _combine(win_start, nsub, yb, pos, x2, mod_l, g, rows_per_batch):
    n, d = x2.shape
    t = CMB_T
    bpb = rows_per_batch // t
    return pl.pallas_call(
        _combine_kernel,
        out_shape=jax.ShapeDtypeStruct((n, d), F32),
        grid_spec=pltpu.PrefetchScalarGridSpec(
            num_scalar_prefetch=2,
            grid=(n // t,),
            in_specs=[
                pl.BlockSpec(memory_space=pl.ANY),
                pl.BlockSpec((t, 2 * TOP_K), lambda i, ws, ns: (i, 0)),
                pl.BlockSpec((t, d), lambda i, ws, ns: (i, 0)),
                pl.BlockSpec((None, 6, d), lambda i, ws, ns: (i // bpb, 0, 0)),
                pl.BlockSpec((1, d), lambda i, ws, ns: (0, 0)),
            ],
            out_specs=pl.BlockSpec((t, d), lambda i, ws, ns: (i, 0)),
            scratch_shapes=[
                pltpu.VMEM((2, N_EXPERTS, CMB_WIN, d), BF16),
                pltpu.SemaphoreType.DMA((2, N_EXPERTS)),
                pltpu.VMEM((CMB_WIN, d), BF16),
                pltpu.SemaphoreType.DMA(()),
                pltpu.VMEM((t, d), F32),
            ],
        ),
        compiler_params=_params(("arbitrary",)),
        name="moe_combine",
    )(win_start, nsub, yb, pos, x2, mod_l, g)


def _moe(x2, mod_l, gpre, gpost, w_router, w1, w3, w2, rows_per_batch):
    n, d = x2.shape
    ne, tm, dtm, win, t = N_EXPERTS, MOE_TM, DSP_TM, DSP_WIN, CMB_T
    wr_hi, wr_lo = _split2(w_router)
    wr_packed = jnp.zeros((d, LANES), BF16).at[:, 0:ne].set(wr_hi).at[:, ne:2 * ne].set(wr_lo)
    h, route, wcum = _router(x2, mod_l, gpre, wr_packed, rows_per_batch)

    top = route[:, 0:TOP_K].astype(I32)
    gates = route[:, TOP_K:2 * TOP_K]
    rank = route[:, 2 * TOP_K:3 * TOP_K].astype(I32)
    wc = wcum.reshape(n // win, LANES)[:, 0:ne].astype(I32)
    counts = wc[-1]
    padded = (counts + tm - 1) // tm * tm
    pend = jnp.cumsum(padded)
    pstart = pend - padded
    experts = jnp.arange(ne, dtype=I32)
    pos = rank + jnp.sum(jnp.where(top[:, :, None] == experts, pstart, 0), axis=-1)

    n_blocks = (n * TOP_K) // tm + ne + 1
    blk = jnp.arange(n_blocks, dtype=I32) * tm
    block_expert = jnp.minimum(jnp.sum((pend[None, :] <= blk[:, None]).astype(I32), axis=1), ne - 1)
    used = (pend[-1] // tm).astype(I32).reshape(1)

    n_steps = n_blocks * (tm // dtm)
    meta = jnp.concatenate([pstart, counts, pend]).astype(I32)
    pos_w = pos.T.reshape(TOP_K, n // win, win).transpose(1, 0, 2)
    gate_w = gates.T.reshape(TOP_K, n // win, win).transpose(1, 0, 2)
    xb, gate_buf = _dispatch(meta, wc.T.reshape(-1), h, pos_w, gate_w, n_steps)
    yb = _experts(block_expert, used, xb, gate_buf, w1, w3, w2, n_blocks)

    at_block_end = wc[t // win - 1::t // win]
    before_block = jnp.concatenate([jnp.zeros((1, ne), I32), at_block_end[:-1]], axis=0)
    base_b = pstart[None, :] + before_block
    end_b = pstart[None, :] + at_block_end
    win_start = base_b // BF16_ROWS * BF16_ROWS
    nsub = jnp.where(end_b > base_b, (end_b - win_start + CMB_WIN - 1) // CMB_WIN, 0)
    return _combine(win_start.astype(I32).reshape(-1), nsub.astype(I32).reshape(-1), yb,
                    jnp.concatenate([pos, top], axis=1), x2, mod_l, gpost, rows_per_batch)


def _reorder_w_in(w):
    main = jnp.concatenate([w[:, 0:3072], w[:, 3080:4616]], axis=1).astype(BF16)
    wdt = jnp.zeros((w.shape[0], LANES), BF16).at[:, 0:SSD_HEADS].set(w[:, 3072:3080].astype(BF16))
    return main, wdt


def _pad_row(v, width=LANES):
    return jnp.zeros((1, width), F32).at[0, 0:v.shape[0]].set(v)


def kernel(x, c, w_mod, b_mod, norm_mix_pre, norm_mix_post, norm_ffn_pre, norm_ffn_post, w_in, w_out, lambda_qk, attn_subln, ssd_conv_w, ssd_conv_b, ssd_dt_bias, ssd_a_log, ssd_d, ssd_norm, sconv_w, ffn_w1, ffn_w3, ffn_w2, moe_router, moe_w1, moe_w3, moe_w2):
    batch, seq, d = x.shape
    depth = w_mod.shape[0]
    n = batch * seq
    assert seq % SSD_CHUNK == 0 and w_in.shape[2] == PROJ_COLS + SSD_HEADS
    mod = _modulation(c, w_mod, b_mod).reshape(depth, batch, 6, d)
    x2 = x.reshape(n, d)
    for i in range(depth):
        mod_l = mod[i]
        w_main, w_dt = _reorder_w_in(w_in[i])
        proj, dt_raw = _inproj(x2, mod_l, norm_mix_pre[i][None, :], w_main, w_dt, seq)
        lam_init = 0.8 - 0.6 * math.exp(-0.3 * i)
        attn = _attention(proj, lambda_qk[i], attn_subln[i][None, :], batch, seq, lam_init)
        sc = _ssd_sconv(proj, dt_raw, ssd_conv_w[i], ssd_conv_b[i][None, :], _pad_row(ssd_dt_bias[i]),
                        _pad_row(ssd_a_log[i]), jnp.repeat(ssd_d[i], SSD_HEAD_DIM)[None, :],
                        ssd_norm[i][None, :], sconv_w[i], batch, seq)
        x2 = _outproj(attn, sc, w_out[i].astype(BF16), x2, mod_l, norm_mix_post[i][None, :], seq)
        if i % 2 == 0:
            x2 = _ffn(x2, mod_l, norm_ffn_pre[i][None, :], norm_ffn_post[i][None, :],
                      ffn_w1[i // 2].astype(BF16), ffn_w3[i // 2].astype(BF16), ffn_w2[i // 2].astype(BF16), seq)
        else:
            x2 = _moe(x2, mod_l, norm_ffn_pre[i][None, :], norm_ffn_post[i][None, :], moe_router[i // 2],
                      moe_w1[i // 2].astype(BF16), moe_w3[i // 2].astype(BF16), moe_w2[i // 2].astype(BF16), seq)
    return x2.reshape(batch, seq, d)
```

```python
import functools
import math

import jax
import jax.numpy as jnp
from jax import lax
from jax.experimental import pallas as pl
from jax.experimental.pallas import tpu as pltpu

F32 = jnp.float32
BF16 = jnp.bfloat16
I32 = jnp.int32

EPS = 1e-6
GROUP_WIDTH = 512
ATTN_HEADS = 4
ATTN_HEAD_DIM = 64
SSD_HEAD_DIM = 64
SSD_HEADS = 8
SSD_GROUPS = 2
SSD_STATE = 128
SSD_CONV = 4
SSD_CHUNK = 128
SSD_XBC = GROUP_WIDTH + 2 * SSD_GROUPS * SSD_STATE
SC_CONV = 3
N_EXPERTS = 8
TOP_K = 2

LANES = 128
SUBLANES = 8
BF16_ROWS = 16
VMEM_LIMIT = 48 * 1024 * 1024
NEG_BIG = -1e30
ROW_SPLIT = 2
RESIDENT = dict(pipeline_mode=pl.Buffered(1))

COL_Q, COL_K, COL_V, COL_Z, COL_XBC, COL_GB, COL_GC, COL_U = 0, 512, 1024, 1536, 2048, 3072, 3584, 4096
PROJ_COLS = 4608

NT_DIMS = (((1,), (1,)), ((), ()))


def _dot(a, b):
    return jnp.dot(a, b, preferred_element_type=F32)


def _dot_nt(a, b):
    return lax.dot_general(a, b, NT_DIMS, preferred_element_type=F32)


def _split2(x):
    hi = x.astype(BF16)
    lo = (x - hi.astype(F32)).astype(BF16)
    return hi, lo


def _split3(x):
    hi = x.astype(BF16)
    r = x - hi.astype(F32)
    mid = r.astype(BF16)
    lo = (r - mid.astype(F32)).astype(BF16)
    return hi, mid, lo


def _sigmoid(x):
    return 0.5 * jnp.tanh(0.5 * x) + 0.5


def _silu(x):
    return x * _sigmoid(x)


def _softplus(x):
    return jnp.maximum(x, 0.0) + jnp.log(1.0 + jnp.exp(-jnp.abs(x)))


def _rms(x, g):
    return x * lax.rsqrt(jnp.mean(x * x, axis=-1, keepdims=True) + EPS) * g


def _params(sem):
    return pltpu.CompilerParams(dimension_semantics=sem, vmem_limit_bytes=VMEM_LIMIT)


def _pick(n, pref):
    t = min(n, pref)
    assert n % t == 0, (n, pref)
    return t


def _mod_kernel(c_ref, w_ref, b_ref, o_ref):
    c = c_ref[...]
    ah, al = _split2(_silu(c))
    wh, wl = _split2(w_ref[...])
    o_ref[...] = _dot(ah, wh) + _dot(ah, wl) + _dot(al, wh) + b_ref[...]


def _modulation(c, w_mod, b_mod):
    depth, d, n6 = w_mod.shape
    b = c.shape[0]
    tn = _pick(n6, 1536)
    return pl.pallas_call(
        _mod_kernel,
        out_shape=jax.ShapeDtypeStruct((depth, b, n6), F32),
        grid=(depth, n6 // tn),
        in_specs=[
            pl.BlockSpec((b, d), lambda l, j: (0, 0)),
            pl.BlockSpec((None, d, tn), lambda l, j: (l, 0, j)),
            pl.BlockSpec((None, 1, tn), lambda l, j: (l, 0, j)),
        ],
        out_specs=pl.BlockSpec((None, b, tn), lambda l, j: (l, 0, j)),
        compiler_params=_params(("parallel", "parallel")),
        name="modulation",
    )(c, w_mod, b_mod.reshape(depth, 1, n6))


def _inproj_kernel(x_ref, mod_ref, g_ref, w_ref, wdt_ref, o_ref, dt_ref):
    rows = x_ref.shape[0] // ROW_SPLIT
    for r in range(ROW_SPLIT):
        sl = slice(r * rows, (r + 1) * rows)
        h = _rms(x_ref[sl, :], g_ref[...]) * (1.0 + mod_ref[1:2, :]) + mod_ref[0:1, :]
        hb = h.astype(BF16)
        dt_ref[sl, :] = _dot(hb, wdt_ref[...])
        o_ref[sl, :] = _dot(hb, w_ref[...]).astype(BF16)


def _inproj(x2, mod_l, g, w, wdt, rows_per_batch):
    n, d = x2.shape
    tm = _pick(rows_per_batch, 512)
    bpb = rows_per_batch // tm
    return pl.pallas_call(
        _inproj_kernel,
        out_shape=(jax.ShapeDtypeStruct((n, PROJ_COLS), BF16),
                   jax.ShapeDtypeStruct((n, LANES), F32)),
        grid=(n // tm,),
        in_specs=[
            pl.BlockSpec((tm, d), lambda i: (i, 0)),
            pl.BlockSpec((None, 6, d), lambda i: (i // bpb, 0, 0)),
            pl.BlockSpec((1, d), lambda i: (0, 0)),
            pl.BlockSpec((d, PROJ_COLS), lambda i: (0, 0), **RESIDENT),
            pl.BlockSpec((d, LANES), lambda i: (0, 0), **RESIDENT),
        ],
        out_specs=(pl.BlockSpec((tm, PROJ_COLS), lambda i: (i, 0)),
                   pl.BlockSpec((tm, LANES), lambda i: (i, 0))),
        compiler_params=_params(("parallel",)),
        name="inproj",
    )(x2, mod_l, g, w, wdt)


def _attn_kernel(q_ref, k_ref, v_ref, lq_ref, sub_ref, o_ref, qt_ref, vt_ref, m_ref, acc_ref,
                 s_ref, cmax_ref, *, tq, lam_init):
    i = pl.program_id(2)
    d = ATTN_HEAD_DIM
    n_chunks = vt_ref.shape[0]

    @pl.when(i == 0)
    def _():
        ones_row = jnp.where(lax.broadcasted_iota(I32, (BF16_ROWS, tq), 0) == 0, 1.0, 0.0).astype(BF16)
        for cidx in range(n_chunks):
            vt_ref[cidx, 0:LANES, :] = v_ref[cidx * tq:(cidx + 1) * tq, :].astype(F32).T.astype(BF16)
            vt_ref[cidx, LANES:, :] = ones_row

    qt = (q_ref[...].astype(F32) * (d ** -0.5 * math.log2(math.e))).T
    row = lax.broadcasted_iota(I32, qt.shape, 0)
    qt_ref[:, 0:tq] = jnp.where(row < d, qt, 0.0).astype(BF16)
    qt_ref[:, tq:] = jnp.where(row >= d, qt, 0.0).astype(BF16)
    m_ref[...] = jnp.full(m_ref.shape, NEG_BIG, F32)
    acc_ref[...] = jnp.zeros(acc_ref.shape, F32)

    def scores(j, slot, masked):
        start = pl.multiple_of(j * tq, tq)
        s = _dot(k_ref[pl.ds(start, tq), :], qt_ref[...])
        if masked:
            kk = lax.broadcasted_iota(I32, (tq, tq), 0)
            qq = lax.broadcasted_iota(I32, (tq, tq), 1)
            keep = kk <= qq
            s = jnp.where(jnp.concatenate([keep, keep], axis=1), s, NEG_BIG)
        s_ref[slot] = s
        cmax_ref[slot] = jnp.max(s, axis=0, keepdims=True)

    def accumulate(j, slot):
        m_prev = m_ref[...]
        m_new = jnp.maximum(m_prev, cmax_ref[slot])
        alpha = jnp.exp2(m_prev - m_new)
        p = jnp.exp2(s_ref[slot] - m_new)
        acc_ref[...] = alpha * acc_ref[...] + _dot(vt_ref[j], p.astype(BF16))
        m_ref[...] = m_new

    scores(i, 0, True)

    def step(t, slot):
        scores(t, 1 - slot, False)
        accumulate(jnp.where(t == 0, i, t - 1), slot)

    def body(u, carry):
        step(2 * u, 0)
        step(2 * u + 1, 1)
        return carry

    lax.fori_loop(0, i // 2, body, 0)

    @pl.when(i % 2 == 1)
    def _():
        step(i - 1, 0)
        accumulate(i - 1, 1)

    @pl.when(i % 2 == 0)
    def _():
        accumulate(jnp.where(i == 0, i, i - 1), 0)

    lq = lq_ref[...]
    lam = (jnp.exp(jnp.sum(lq[0:1, :] * lq[1:2, :], axis=-1, keepdims=True))
           - jnp.exp(jnp.sum(lq[2:3, :] * lq[3:4, :], axis=-1, keepdims=True)) + lam_init)
    inv = 1.0 / acc_ref[LANES:LANES + 1, :]
    ot = acc_ref[0:LANES, 0:tq] * inv[:, 0:tq] - lam * (acc_ref[0:LANES, tq:] * inv[:, tq:])
    ot = ot * lax.rsqrt(jnp.mean(ot * ot, axis=0, keepdims=True) + EPS)
    o_ref[...] = (ot.T * sub_ref[...] * (1.0 - lam_init)).astype(BF16)


def _attention(proj, lambda_qk, subln, batch, seq, lam_init):
    n = proj.shape[0]
    tq = _pick(seq, 512)
    nq = seq // tq
    h = ATTN_HEADS
    return pl.pallas_call(
        functools.partial(_attn_kernel, tq=tq, lam_init=lam_init),
        out_shape=jax.ShapeDtypeStruct((n, GROUP_WIDTH), BF16),
        grid=(batch, h, nq),
        in_specs=[
            pl.BlockSpec((tq, LANES), lambda b, hh, i: (b * nq + i, COL_Q // LANES + hh)),
            pl.BlockSpec((seq, LANES), lambda b, hh, i: (b, COL_K // LANES + hh)),
            pl.BlockSpec((seq, LANES), lambda b, hh, i: (b, COL_V // LANES + hh)),
            pl.BlockSpec((4, ATTN_HEAD_DIM), lambda b, hh, i: (0, 0)),
            pl.BlockSpec((1, LANES), lambda b, hh, i: (0, 0)),
        ],
        out_specs=pl.BlockSpec((tq, LANES), lambda b, hh, i: (b * nq + i, hh)),
        scratch_shapes=[
            pltpu.VMEM((LANES, 2 * tq), BF16),
            pltpu.VMEM((nq, LANES + BF16_ROWS, tq), BF16),
            pltpu.VMEM((1, 2 * tq), F32),
            pltpu.VMEM((LANES + BF16_ROWS, 2 * tq), F32),
            pltpu.VMEM((2, tq, 2 * tq), F32),
            pltpu.VMEM((2, 1, 2 * tq), F32),
        ],
        compiler_params=_params(("parallel", "parallel", "arbitrary")),
        name="diff_attention",
    )(proj, proj, proj, lambda_qk, subln)


def _causal_conv(x, carry, w, width):
    row = lax.broadcasted_iota(I32, carry.shape, 0)
    out = x * w[width - 1:width, :]
    for k in range(1, width):
        xr = pltpu.roll(x, k, 0)
        cr = pltpu.roll(carry, k, 0)
        head = jnp.where(row < k, cr, xr[0:SUBLANES, :])
        xk = jnp.concatenate([head, xr[SUBLANES:, :]], axis=0)
        out = out + xk * w[width - 1 - k:width - k, :]
    return out


def _ssd_kernel(z_ref, xbc_ref, gb_ref, gc_ref, u_ref, dt_ref, cw_ref, cb_ref, dtb_ref, alog_ref,
                dskip_ref, ng_ref, sw_ref, o_ref,
                cx_ref, cs_ref, state_ref, xs_ref, bm_ref, cm_ref, *, tc):
    L = SSD_CHUNK
    gw = GROUP_WIDTH
    ns = SSD_STATE

    @pl.when(pl.program_id(1) == 0)
    def _():
        cx_ref[...] = jnp.zeros(cx_ref.shape, F32)
        cs_ref[...] = jnp.zeros(cs_ref.shape, F32)
        state_ref[...] = jnp.zeros(state_ref.shape, F32)

    pu = gc_ref[...].astype(F32) * u_ref[...].astype(F32)
    sconv = gb_ref[...].astype(F32) * _causal_conv(pu, cs_ref[...], sw_ref[...], SC_CONV)
    cs_ref[...] = pu[tc - SUBLANES:, :]
    o_ref[:, gw:] = sconv.astype(BF16)

    xbc = xbc_ref[...].astype(F32)
    act = _silu(_causal_conv(xbc, cx_ref[...], cw_ref[...], SSD_CONV) + cb_ref[...])
    cx_ref[...] = xbc[tc - SUBLANES:, :]
    xs_ref[...] = act[:, 0:gw]
    bm_ref[...] = act[:, gw:gw + SSD_GROUPS * ns]
    cm_ref[...] = act[:, gw + SSD_GROUPS * ns:]

    lane = lax.broadcasted_iota(I32, (L, LANES), 1)
    lo = lane < SSD_HEAD_DIM
    rr = lax.broadcasted_iota(I32, (L, L), 0)
    cc = lax.broadcasted_iota(I32, (L, L), 1)
    tril = cc <= rr
    ltri = jnp.where(tril, 1.0, 0.0).astype(BF16)
    head_lane = lane < SSD_HEADS
    a_row = -jnp.exp(alog_ref[...])

    def pair_pattern(mat, h0):
        lo_b = lo[0:mat.shape[0], :]
        return jnp.where(lo_b, mat[:, h0:h0 + 1], mat[:, h0 + 1:h0 + 2])

    def chunk_body(c, carry):
        r0 = pl.multiple_of(c * L, L)
        rows = pl.ds(r0, L)
        dt = _softplus(dt_ref[rows, :] + dtb_ref[...])
        da = jnp.where(head_lane, dt * a_row, 0.0)
        d_hi, d_mid, d_lo = _split3(da)
        acum = _dot(ltri, d_hi) + _dot(ltri, d_mid) + _dot(ltri, d_lo)
        acum_t = acum.T
        a_last = acum[L - 1:L, :]
        ys = []
        for g in range(SSD_GROUPS):
            bg = bm_ref[rows, g * ns:(g + 1) * ns]
            cg = cm_ref[rows, g * ns:(g + 1) * ns].astype(BF16)
            cb = _dot_nt(cg, bg.astype(BF16))
            bg_t = bg.T.astype(BF16)
            for jp in range(SSD_HEADS // SSD_GROUPS // 2):
                j = g * (SSD_HEADS // SSD_GROUPS // 2) + jp
                h0 = 2 * j
                xs = xs_ref[rows, j * LANES:(j + 1) * LANES]
                xdt = xs * pair_pattern(dt, h0)
                y = None
                for hh, keep in ((h0, lo), (h0 + 1, jnp.logical_not(lo))):
                    seg = acum[:, hh:hh + 1] - acum_t[hh:hh + 1, :]
                    dec = jnp.exp(jnp.where(tril, seg, NEG_BIG))
                    mm = (dec * cb).astype(BF16)
                    part = _dot(mm, jnp.where(keep, xdt, 0.0).astype(BF16))
                    y = part if y is None else y + part
                st = state_ref[j]
                y = y + _dot(cg, st.astype(BF16)) * jnp.exp(pair_pattern(acum, h0))
                dte = jnp.exp(pair_pattern(a_last - acum, h0))
                contrib = _dot(bg_t, (xdt * dte).astype(BF16))
                state_ref[j] = st * jnp.exp(pair_pattern(a_last, h0)) + contrib
                ys.append(y + dskip_ref[:, j * LANES:(j + 1) * LANES] * xs)
        yv = jnp.concatenate(ys, axis=-1)
        z = z_ref[rows, :].astype(F32)
        vv = yv * _silu(z)
        gwid = gw // SSD_GROUPS
        outs = []
        for g in range(SSD_GROUPS):
            vg = vv[:, g * gwid:(g + 1) * gwid]
            outs.append(vg * lax.rsqrt(jnp.mean(vg * vg, axis=-1, keepdims=True) + EPS))
        o_ref[rows, 0:gw] = (jnp.concatenate(outs, axis=-1) * ng_ref[...]).astype(BF16)
        return carry

    lax.fori_loop(0, tc // L, chunk_body, 0)


def _ssd_sconv(proj, dt_raw, cw, cb, dtb, alog, dskip, ng, sw, batch, seq):
    n = proj.shape[0]
    tc = _pick(seq, 512)
    nt = seq // tc
    row = lambda b, t: b * nt + t
    const = lambda b, t: (0, 0)
    return pl.pallas_call(
        functools.partial(_ssd_kernel, tc=tc),
        out_shape=jax.ShapeDtypeStruct((n, 2 * GROUP_WIDTH), BF16),
        grid=(batch, nt),
        in_specs=[
            pl.BlockSpec((tc, GROUP_WIDTH), lambda b, t: (row(b, t), COL_Z // GROUP_WIDTH)),
            pl.BlockSpec((tc, SSD_XBC), lambda b, t: (row(b, t), COL_XBC // SSD_XBC)),
            pl.BlockSpec((tc, GROUP_WIDTH), lambda b, t: (row(b, t), COL_GB // GROUP_WIDTH)),
            pl.BlockSpec((tc, GROUP_WIDTH), lambda b, t: (row(b, t), COL_GC // GROUP_WIDTH)),
            pl.BlockSpec((tc, GROUP_WIDTH), lambda b, t: (row(b, t), COL_U // GROUP_WIDTH)),
            pl.BlockSpec((tc, LANES), lambda b, t: (row(b, t), 0)),
            pl.BlockSpec((SSD_CONV, SSD_XBC), const),
            pl.BlockSpec((1, SSD_XBC), const),
            pl.BlockSpec((1, LANES), const),
            pl.BlockSpec((1, LANES), const),
            pl.BlockSpec((1, GROUP_WIDTH), const),
            pl.BlockSpec((1, GROUP_WIDTH), const),
            pl.BlockSpec((SC_CONV, GROUP_WIDTH), const),
        ],
        out_specs=pl.BlockSpec((tc, 2 * GROUP_WIDTH), lambda b, t: (row(b, t), 0)),
        scratch_shapes=[
            pltpu.VMEM((SUBLANES, SSD_XBC), F32),
            pltpu.VMEM((SUBLANES, GROUP_WIDTH), F32),
            pltpu.VMEM((SSD_HEADS // 2, SSD_STATE, LANES), F32),
            pltpu.VMEM((tc, GROUP_WIDTH), F32),
            pltpu.VMEM((tc, SSD_GROUPS * SSD_STATE), F32),
            pltpu.VMEM((tc, SSD_GROUPS * SSD_STATE), F32),
        ],
        compiler_params=_params(("parallel", "arbitrary")),
        name="ssd_sconv",
    )(proj, proj, proj, proj, proj, dt_raw, cw, cb, dtb, alog, dskip, ng, sw)


def _outproj_kernel(a_ref, sc_ref, w_ref, x_ref, mod_ref, g_ref, o_ref):
    gw = GROUP_WIDTH
    rows = x_ref.shape[0] // ROW_SPLIT
    for r in range(ROW_SPLIT):
        sl = slice(r * rows, (r + 1) * rows)
        y = _dot(a_ref[sl, :], w_ref[0:gw, :]) + _dot(sc_ref[sl, :], w_ref[gw:, :])
        o_ref[sl, :] = x_ref[sl, :] + mod_ref[2:3, :] * _rms(y, g_ref[...])


def _outproj(attn, sc, w, x2, mod_l, g, rows_per_batch):
    n, d = x2.shape
    tm = _pick(rows_per_batch, 512)
    bpb = rows_per_batch // tm
    return pl.pallas_call(
        _outproj_kernel,
        out_shape=jax.ShapeDtypeStruct((n, d), F32),
        grid=(n // tm,),
        in_specs=[
            pl.BlockSpec((tm, GROUP_WIDTH), lambda i: (i, 0)),
            pl.BlockSpec((tm, 2 * GROUP_WIDTH), lambda i: (i, 0)),
            pl.BlockSpec((3 * GROUP_WIDTH, d), lambda i: (0, 0), **RESIDENT),
            pl.BlockSpec((tm, d), lambda i: (i, 0)),
            pl.BlockSpec((None, 6, d), lambda i: (i // bpb, 0, 0)),
            pl.BlockSpec((1, d), lambda i: (0, 0)),
        ],
        out_specs=pl.BlockSpec((tm, d), lambda i: (i, 0)),
        compiler_params=_params(("parallel",)),
        name="outproj",
    )(attn, sc, w, x2, mod_l, g)


def _ffn_kernel(x_ref, mod_ref, gpre_ref, gpost_ref, w1_ref, w3_ref, w2_ref, o_ref):
    rows = x_ref.shape[0] // ROW_SPLIT
    for r in range(ROW_SPLIT):
        sl = slice(r * rows, (r + 1) * rows)
        x = x_ref[sl, :]
        h = (_rms(x, gpre_ref[...]) * (1.0 + mod_ref[4:5, :]) + mod_ref[3:4, :]).astype(BF16)
        t = (_silu(_dot(h, w1_ref[...])) * _dot(h, w3_ref[...])).astype(BF16)
        y = _dot(t, w2_ref[...])
        o_ref[sl, :] = x + mod_ref[5:6, :] * _rms(y, gpost_ref[...])


def _ffn(x2, mod_l, gpre, gpost, w1, w3, w2, rows_per_batch):
    n, d = x2.shape
    dff = w1.shape[1]
    tm = _pick(rows_per_batch, 512)
    bpb = rows_per_batch // tm
    return pl.pallas_call(
        _ffn_kernel,
        out_shape=jax.ShapeDtypeStruct((n, d), F32),
        grid=(n // tm,),
        in_specs=[
            pl.BlockSpec((tm, d), lambda i: (i, 0)),
            pl.BlockSpec((None, 6, d), lambda i: (i // bpb, 0, 0)),
            pl.BlockSpec((1, d), lambda i: (0, 0)),
            pl.BlockSpec((1, d), lambda i: (0, 0)),
            pl.BlockSpec((d, dff), lambda i: (0, 0), **RESIDENT),
            pl.BlockSpec((d, dff), lambda i: (0, 0), **RESIDENT),
            pl.BlockSpec((dff, d), lambda i: (0, 0), **RESIDENT),
        ],
        out_specs=pl.BlockSpec((tm, d), lambda i: (i, 0)),
        compiler_params=_params(("parallel",)),
        name="dense_ffn",
    )(x2, mod_l, gpre, gpost, w1, w3, w2)


MOE_TM = 512
DSP_TM = 256
DSP_WIN = 256
DSP_DEPTH = 4
CMB_T = 512
CMB_WIN = 256
CMB_MAX_SUB = (CMB_T + BF16_ROWS + CMB_WIN - 1) // CMB_WIN


def _router_kernel(x_ref, mod_ref, g_ref, wr_ref, h_ref, route_ref, wcum_ref, run_ref):
    @pl.when(pl.program_id(0) == 0)
    def _():
        run_ref[...] = jnp.zeros(run_ref.shape, F32)

    h = _rms(x_ref[...], g_ref[...]) * (1.0 + mod_ref[4:5, :]) + mod_ref[3:4, :]
    hh, hl = _split2(h)
    h_ref[...] = hh
    a = _dot(hh, wr_ref[...])
    b = _dot(hl, wr_ref[...])
    ne = N_EXPERTS
    logits = a + b + pltpu.roll(a, LANES - ne, 1)
    lane = lax.broadcasted_iota(I32, logits.shape, 1)
    valid = lane < ne
    l1 = jnp.max(jnp.where(valid, logits, NEG_BIG), axis=-1, keepdims=True)
    e1 = jnp.min(jnp.where(valid & (logits == l1), lane, LANES), axis=-1, keepdims=True)
    rest = valid & (lane != e1)
    l2 = jnp.max(jnp.where(rest, logits, NEG_BIG), axis=-1, keepdims=True)
    e2 = jnp.min(jnp.where(rest & (logits == l2), lane, LANES), axis=-1, keepdims=True)
    w = jnp.exp(l2 - l1)
    g1 = 1.0 / (1.0 + w)
    g2 = w / (1.0 + w)

    tm = logits.shape[0]
    sel = jnp.where((lane == e1) | (lane == e2), 1.0, 0.0)
    rr = lax.broadcasted_iota(I32, (tm, tm), 0)
    cc = lax.broadcasted_iota(I32, (tm, tm), 1)
    before = jnp.where(cc < rr, 1.0, 0.0).astype(BF16)
    excl = _dot(before, sel.astype(BF16)) + run_ref[...]
    rank1 = jnp.sum(jnp.where(lane == e1, excl, 0.0), axis=-1, keepdims=True)
    rank2 = jnp.sum(jnp.where(lane == e2, excl, 0.0), axis=-1, keepdims=True)
    total = excl + sel
    for wdx in range(tm // DSP_WIN):
        wcum_ref[wdx:wdx + 1, :] = total[(wdx + 1) * DSP_WIN - 1:(wdx + 1) * DSP_WIN, :]
    run_ref[...] = total[tm - 1:tm, :]

    lane8 = lax.broadcasted_iota(I32, route_ref.shape, 1)
    cols = (e1.astype(F32), e2.astype(F32), g1, g2, rank1, rank2)
    out = jnp.zeros(route_ref.shape, F32)
    for idx, col in enumerate(cols):
        out = jnp.where(lane8 == idx, col, out)
    route_ref[...] = out


def _router(x2, mod_l, g, wr_packed, rows_per_batch):
    n, d = x2.shape
    tm = _pick(rows_per_batch, 512)
    assert tm % DSP_WIN == 0
    bpb = rows_per_batch // tm
    return pl.pallas_call(
        _router_kernel,
        out_shape=(jax.ShapeDtypeStruct((n, d), BF16), jax.ShapeDtypeStruct((n, SUBLANES), F32),
                   jax.ShapeDtypeStruct((n // tm, tm // DSP_WIN, LANES), F32)),
        grid=(n // tm,),
        in_specs=[
            pl.BlockSpec((tm, d), lambda i: (i, 0)),
            pl.BlockSpec((None, 6, d), lambda i: (i // bpb, 0, 0)),
            pl.BlockSpec((1, d), lambda i: (0, 0)),
            pl.BlockSpec((d, LANES), lambda i: (0, 0)),
        ],
        out_specs=(pl.BlockSpec((tm, d), lambda i: (i, 0)),
                   pl.BlockSpec((tm, SUBLANES), lambda i: (i, 0)),
                   pl.BlockSpec((None, tm // DSP_WIN, LANES), lambda i: (i, 0, 0))),
        scratch_shapes=[pltpu.VMEM((1, LANES), F32)],
        compiler_params=_params(("arbitrary",)),
        name="moe_router",
    )(x2, mod_l, g, wr_packed)


def _dispatch_kernel(meta_ref, wc_ref, h_hbm, pos_ref, gt_ref, xb_ref, gate_ref,
                     buf_ref, sem_ref, acc_ref, gacc_ref, st_ref):
    b = pl.program_id(0)
    nb = pl.num_programs(0)
    ne = N_EXPERTS
    tm, win = DSP_TM, DSP_WIN
    nslot = DSP_DEPTH + 1
    nwt = pos_ref.shape[0]

    def copy(w, slot):
        return pltpu.make_async_copy(h_hbm.at[pl.ds(pl.multiple_of(w * win, win), win), :],
                                     buf_ref.at[slot], sem_ref.at[slot])

    def step_range(s):
        p = s * tm
        e = jnp.zeros((), I32)
        for j in range(ne - 1):
            e = e + (p >= meta_ref[2 * ne + j]).astype(I32)
        r0 = p - meta_ref[e]
        cnt = meta_ref[ne + e]
        r1 = jnp.minimum(r0 + tm, cnt)

        def first_window_reaching(target):
            def halve(_, c):
                lo, hi = c
                mid = (lo + hi) // 2
                less = wc_ref[e * nwt + mid] < target
                return jnp.where(less, mid + 1, lo), jnp.where(less, hi, mid)

            return lax.fori_loop(0, max(nwt - 1, 1).bit_length(), halve,
                                 (jnp.zeros((), I32), jnp.full((), nwt - 1, I32)))[0]

        w_first = first_window_reaching(r0 + 1)
        w_last = first_window_reaching(r1)
        has = r0 < cnt
        return jnp.where(has, w_first, 0), jnp.where(has, w_last - w_first + 1, 0)

    def produce():
        def exhausted(c):
            return (c[0] < nb) & (c[1] >= c[3])

        def next_step(c):
            s = c[0] + 1
            w_first, count = step_range(jnp.minimum(s, nb - 1))
            return s, jnp.zeros_like(c[1]), w_first, count

        ps, pk, pw, pn = lax.while_loop(exhausted, next_step,
                                        (st_ref[0], st_ref[1], st_ref[4], st_ref[5]))
        st_ref[0] = ps
        st_ref[4] = pw
        st_ref[5] = pn

        @pl.when(ps < nb)
        def _():
            issued = st_ref[2]
            copy(pw + pk, issued % nslot).start()
            st_ref[1] = pk + 1
            st_ref[2] = issued + 1

        @pl.when(ps >= nb)
        def _():
            st_ref[1] = pk

    w0, nw = step_range(b)

    @pl.when(b == 0)
    def _():
        for idx in range(4):
            st_ref[idx] = 0
        st_ref[4] = w0
        st_ref[5] = nw
        for _ in range(DSP_DEPTH):
            produce()

    g0 = st_ref[3]
    acc_ref[...] = jnp.zeros(acc_ref.shape, F32)
    gacc_ref[...] = jnp.zeros(gacc_ref.shape, F32)
    slot_id = b * tm + lax.broadcasted_iota(I32, (tm, win), 0)

    def body(k, carry):
        slot = (g0 + k) % nslot
        copy(w0 + k, slot).wait()
        produce()
        pos = pos_ref[w0 + k]
        gt = gt_ref[w0 + k]
        eq0 = pos[0:1, :] == slot_id
        eq1 = pos[1:2, :] == slot_id
        onehot = jnp.where(eq0 | eq1, 1.0, 0.0).astype(BF16)
        acc_ref[...] += _dot(onehot, buf_ref[slot])
        gsel = jnp.where(eq0, gt[0:1, :], 0.0) + jnp.where(eq1, gt[1:2, :], 0.0)
        gacc_ref[...] += jnp.sum(gsel, axis=-1, keepdims=True)
        return carry

    lax.fori_loop(0, nw, body, 0)
    st_ref[3] = g0 + nw
    xb_ref[...] = acc_ref[...].astype(BF16)
    gate_ref[...] = gacc_ref[...]


def _dispatch(meta, wcum_flat, h, pos_w, gate_w, n_steps):
    n, d = h.shape
    tm, win = DSP_TM, DSP_WIN
    nwt = n // win
    return pl.pallas_call(
        _dispatch_kernel,
        out_shape=(jax.ShapeDtypeStruct((n_steps * tm, d), BF16),
                   jax.ShapeDtypeStruct((n_steps * tm, 1), F32)),
        grid_spec=pltpu.PrefetchScalarGridSpec(
            num_scalar_prefetch=2,
            grid=(n_steps,),
            in_specs=[
                pl.BlockSpec(memory_space=pl.ANY),
                pl.BlockSpec((nwt, 2, win), lambda b, w0, nw: (0, 0, 0)),
                pl.BlockSpec((nwt, 2, win), lambda b, w0, nw: (0, 0, 0)),
            ],
            out_specs=(pl.BlockSpec((tm, d), lambda b, w0, nw: (b, 0)),
                       pl.BlockSpec((tm, 1), lambda b, w0, nw: (b, 0))),
            scratch_shapes=[
                pltpu.VMEM((DSP_DEPTH + 1, win, d), BF16),
                pltpu.SemaphoreType.DMA((DSP_DEPTH + 1,)),
                pltpu.VMEM((tm, d), F32),
                pltpu.VMEM((tm, 1), F32),
                pltpu.SMEM((6,), I32),
            ],
        ),
        compiler_params=_params(("arbitrary",)),
        name="moe_dispatch",
    )(meta, wcum_flat, h, pos_w, gate_w)


def _expert_kernel(be_ref, used_ref, xb_ref, gate_ref, w1_ref, w3_ref, w2_ref, yb_ref, acc_ref):
    b = pl.program_id(0)
    j = pl.program_id(1)
    last = pl.num_programs(1) - 1
    live = b < used_ref[0]

    @pl.when((b == 0) & (j == 0))
    def _():
        acc_ref[...] = jnp.zeros(acc_ref.shape, F32)

    @pl.when(live)
    def _():
        rows = xb_ref.shape[0] // ROW_SPLIT
        for r in range(ROW_SPLIT):
            sl = slice(r * rows, (r + 1) * rows)
            xb = xb_ref[sl, :]
            t = (_silu(_dot(xb, w1_ref[...])) * _dot(xb, w3_ref[...])).astype(BF16)
            part = _dot(t, w2_ref[...])
            acc_ref[sl, :] = jnp.where(j == 0, 0.0, acc_ref[sl, :]) + part

        @pl.when(j == last)
        def _():
            yb_ref[...] = (acc_ref[...] * gate_ref[...]).astype(BF16)

    @pl.when(jnp.logical_not(live) & (j == last))
    def _():
        yb_ref[...] = jnp.zeros(yb_ref.shape, BF16)


def _experts(block_expert, used, xb, gate_buf, w1, w3, w2, n_blocks):
    d = xb.shape[1]
    dff = w1.shape[2]
    tm = MOE_TM
    tf = dff // 2 if (dff // 2) % LANES == 0 else dff
    nj = dff // tf

    def jj(b, j, be, used):
        return jnp.where(b < used[0], j, nj - 1)

    return pl.pallas_call(
        _expert_kernel,
        out_shape=jax.ShapeDtypeStruct((n_blocks * tm, d), BF16),
        grid_spec=pltpu.PrefetchScalarGridSpec(
            num_scalar_prefetch=2,
            grid=(n_blocks, nj),
            in_specs=[
                pl.BlockSpec((tm, d), lambda b, j, be, used: (b, 0)),
                pl.BlockSpec((tm, 1), lambda b, j, be, used: (b, 0)),
                pl.BlockSpec((None, d, tf), lambda b, j, be, used: (be[b], 0, jj(b, j, be, used))),
                pl.BlockSpec((None, d, tf), lambda b, j, be, used: (be[b], 0, jj(b, j, be, used))),
                pl.BlockSpec((None, tf, d), lambda b, j, be, used: (be[b], jj(b, j, be, used), 0)),
            ],
            out_specs=pl.BlockSpec((tm, d), lambda b, j, be, used: (b, 0)),
            scratch_shapes=[pltpu.VMEM((tm, d), F32)],
        ),
        compiler_params=_params(("arbitrary", "arbitrary")),
        name="moe_experts",
    )(block_expert, used, xb, gate_buf, w1, w3, w2)


def _combine_kernel(ws_ref, nsub_ref, yb_hbm, pos_ref, x_ref, mod_ref, g_ref, o_ref,
                    buf_ref, sem_ref, xbuf_ref, xsem_ref, y_ref):
    i = pl.program_id(0)
    n_steps = pl.num_programs(0)
    ne = N_EXPERTS
    t, cw = CMB_T, CMB_WIN

    def start_row(step, e, sub):
        return pl.multiple_of(ws_ref[step * ne + e] + sub * cw, BF16_ROWS)

    def copy(step, e, slot):
        return pltpu.make_async_copy(yb_hbm.at[pl.ds(start_row(step, e, 0), cw), :],
                                     buf_ref.at[slot, e], sem_ref.at[slot, e])

    @pl.when(i == 0)
    def _():
        for e in range(ne):
            copy(0, e, 0).start()

    slot = i % 2

    @pl.when(i + 1 < n_steps)
    def _():
        for e in range(ne):
            copy(i + 1, e, 1 - slot).start()

    pe = pos_ref[...]
    pos = pe[:, 0:TOP_K]
    top = pe[:, TOP_K:2 * TOP_K]
    col = lax.broadcasted_iota(I32, (t, cw), 1)

    def onehot(rel):
        return jnp.where((rel[:, 0:1] == col) | (rel[:, 1:2] == col), 1.0, 0.0).astype(BF16)

    rels = []
    for e in range(ne):
        copy(i, e, slot).wait()
        rels.append(jnp.where(top == e, pos - ws_ref[i * ne + e], -1))
    picks = jnp.concatenate([onehot(rel) for rel in rels], axis=1)
    y_ref[...] = _dot(picks, buf_ref[slot].reshape(ne * cw, buf_ref.shape[-1]))

    for e in range(ne):
        nsub = nsub_ref[i * ne + e]
        for sub in range(1, CMB_MAX_SUB):
            @pl.when(nsub > sub)
            def _():
                extra = pltpu.make_async_copy(yb_hbm.at[pl.ds(start_row(i, e, sub), cw), :],
                                              xbuf_ref, xsem_ref)
                extra.start()
                extra.wait()
                y_ref[...] += _dot(onehot(rels[e] - sub * cw), xbuf_ref[...])

    o_ref[...] = x_ref[...] + mod_ref[5:6, :] * _rms(y_ref[...], g_ref[...])


def _combine(win_start, nsub, yb, pos, x2, mod_l, g, rows_per_batch):
    n, d = x2.shape
    t = CMB_T
    bpb = rows_per_batch // t
    return pl.pallas_call(
        _combine_kernel,
        out_shape=jax.ShapeDtypeStruct((n, d), F32),
        grid_spec=pltpu.PrefetchScalarGridSpec(
            num_scalar_prefetch=2,
            grid=(n // t,),
            in_specs=[
                pl.BlockSpec(memory_space=pl.ANY),
                pl.BlockSpec((t, 2 * TOP_K), lambda i, ws, ns: (i, 0)),
                pl.BlockSpec((t, d), lambda i, ws, ns: (i, 0)),
                pl.BlockSpec((None, 6, d), lambda i, ws, ns: (i // bpb, 0, 0)),
                pl.BlockSpec((1, d), lambda i, ws, ns: (0, 0)),
            ],
            out_specs=pl.BlockSpec((t, d), lambda i, ws, ns: (i, 0)),
            scratch_shapes=[
                pltpu.VMEM((2, N_EXPERTS, CMB_WIN, d), BF16),
                pltpu.SemaphoreType.DMA((2, N_EXPERTS)),
                pltpu.VMEM((CMB_WIN, d), BF16),
                pltpu.SemaphoreType.DMA(()),
                pltpu.VMEM((t, d), F32),
            ],
        ),
        compiler_params=_params(("arbitrary",)),
        name="moe_combine",
    )(win_start, nsub, yb, pos, x2, mod_l, g)


def _moe(x2, mod_l, gpre, gpost, w_router, w1, w3, w2, rows_per_batch):
    n, d = x2.shape
    ne, tm, dtm, win, t = N_EXPERTS, MOE_TM, DSP_TM, DSP_WIN, CMB_T
    wr_hi, wr_lo = _split2(w_router)
    wr_packed = jnp.zeros((d, LANES), BF16).at[:, 0:ne].set(wr_hi).at[:, ne:2 * ne].set(wr_lo)
    h, route, wcum = _router(x2, mod_l, gpre, wr_packed, rows_per_batch)

    top = route[:, 0:TOP_K].astype(I32)
    gates = route[:, TOP_K:2 * TOP_K]
    rank = route[:, 2 * TOP_K:3 * TOP_K].astype(I32)
    wc = wcum.reshape(n // win, LANES)[:, 0:ne].astype(I32)
    counts = wc[-1]
    padded = (counts + tm - 1) // tm * tm
    pend = jnp.cumsum(padded)
    pstart = pend - padded
    experts = jnp.arange(ne, dtype=I32)
    pos = rank + jnp.sum(jnp.where(top[:, :, None] == experts, pstart, 0), axis=-1)

    n_blocks = (n * TOP_K) // tm + ne + 1
    blk = jnp.arange(n_blocks, dtype=I32) * tm
    block_expert = jnp.minimum(jnp.sum((pend[None, :] <= blk[:, None]).astype(I32), axis=1), ne - 1)
    used = (pend[-1] // tm).astype(I32).reshape(1)

    n_steps = n_blocks * (tm // dtm)
    meta = jnp.concatenate([pstart, counts, pend]).astype(I32)
    pos_w = pos.T.reshape(TOP_K, n // win, win).transpose(1, 0, 2)
    gate_w = gates.T.reshape(TOP_K, n // win, win).transpose(1, 0, 2)
    xb, gate_buf = _dispatch(meta, wc.T.reshape(-1), h, pos_w, gate_w, n_steps)
    yb = _experts(block_expert, used, xb, gate_buf, w1, w3, w2, n_blocks)

    at_block_end = wc[t // win - 1::t // win]
    before_block = jnp.concatenate([jnp.zeros((1, ne), I32), at_block_end[:-1]], axis=0)
    base_b = pstart[None, :] + before_block
    end_b = pstart[None, :] + at_block_end
    win_start = base_b // BF16_ROWS * BF16_ROWS
    nsub = jnp.where(end_b > base_b, (end_b - win_start + CMB_WIN - 1) // CMB_WIN, 0)
    return _combine(win_start.astype(I32).reshape(-1), nsub.astype(I32).reshape(-1), yb,
                    jnp.concatenate([pos, top], axis=1), x2, mod_l, gpost, rows_per_batch)


def _reorder_w_in(w):
    main = jnp.concatenate([w[:, 0:3072], w[:, 3080:4616]], axis=1).astype(BF16)
    wdt = jnp.zeros((w.shape[0], LANES), BF16).at[:, 0:SSD_HEADS].set(w[:, 3072:3080].astype(BF16))
    return main, wdt


def _pad_row(v, width=LANES):
    return jnp.zeros((1, width), F32).at[0, 0:v.shape[0]].set(v)


def kernel(x, c, w_mod, b_mod, norm_mix_pre, norm_mix_post, norm_ffn_pre, norm_ffn_post, w_in, w_out, lambda_qk, attn_subln, ssd_conv_w, ssd_conv_b, ssd_dt_bias, ssd_a_log, ssd_d, ssd_norm, sconv_w, ffn_w1, ffn_w3, ffn_w2, moe_router, moe_w1, moe_w3, moe_w2):
    batch, seq, d = x.shape
    depth = w_mod.shape[0]
    n = batch * seq
    assert seq % SSD_CHUNK == 0 and w_in.shape[2] == PROJ_COLS + SSD_HEADS
    mod = _modulation(c, w_mod, b_mod).reshape(depth, batch, 6, d)
    x2 = x.reshape(n, d)
    for i in range(depth):
        mod_l = mod[i]
        w_main, w_dt = _reorder_w_in(w_in[i])
        proj, dt_raw = _inproj(x2, mod_l, norm_mix_pre[i][None, :], w_main, w_dt, seq)
        lam_init = 0.8 - 0.6 * math.exp(-0.3 * i)
        attn = _attention(proj, lambda_qk[i], attn_subln[i][None, :], batch, seq, lam_init)
        sc = _ssd_sconv(proj, dt_raw, ssd_conv_w[i], ssd_conv_b[i][None, :], _pad_row(ssd_dt_bias[i]),
                        _pad_row(ssd_a_log[i]), jnp.repeat(ssd_d[i], SSD_HEAD_DIM)[None, :],
                        ssd_norm[i][None, :], sconv_w[i], batch, seq)
        x2 = _outproj(attn, sc, w_out[i].astype(BF16), x2, mod_l, norm_mix_post[i][None, :], seq)
        if i % 2 == 0:
            x2 = _ffn(x2, mod_l, norm_ffn_pre[i][None, :], norm_ffn_post[i][None, :],
                      ffn_w1[i // 2].astype(BF16), ffn_w3[i // 2].astype(BF16), ffn_w2[i // 2].astype(BF16), seq)
        else:
            x2 = _moe(x2, mod_l, norm_ffn_pre[i][None, :], norm_ffn_post[i][None, :], moe_router[i // 2],
                      moe_w1[i // 2].astype(BF16), moe_w3[i // 2].astype(BF16), moe_w2[i // 2].astype(BF16), seq)
    return x2.reshape(batch, seq, d)
```

```python
import functools
import math

import jax
import jax.numpy as jnp
from jax import lax
from jax.experimental import pallas as pl
from jax.experimental.pallas import tpu as pltpu

F32 = jnp.float32
BF16 = jnp.bfloat16
I32 = jnp.int32

EPS = 1e-6
GROUP_WIDTH = 512
ATTN_HEADS = 4
ATTN_HEAD_DIM = 64
SSD_HEAD_DIM = 64
SSD_HEADS = 8
SSD_GROUPS = 2
SSD_STATE = 128
SSD_CONV = 4
SSD_CHUNK = 128
SSD_XBC = GROUP_WIDTH + 2 * SSD_GROUPS * SSD_STATE
SC_CONV = 3
N_EXPERTS = 8
TOP_K = 2

LANES = 128
SUBLANES = 8
BF16_ROWS = 16
VMEM_LIMIT = 48 * 1024 * 1024
NEG_BIG = -1e30
ROW_SPLIT = 2
RESIDENT = dict(pipeline_mode=pl.Buffered(1))

COL_Q, COL_K, COL_V, COL_Z, COL_XBC, COL_GB, COL_GC, COL_U = 0, 512, 1024, 1536, 2048, 3072, 3584, 4096
PROJ_COLS = 4608

NT_DIMS = (((1,), (1,)), ((), ()))


def _dot(a, b):
    return jnp.dot(a, b, preferred_element_type=F32)


def _dot_nt(a, b):
    return lax.dot_general(a, b, NT_DIMS, preferred_element_type=F32)


def _split2(x):
    hi = x.astype(BF16)
    lo = (x - hi.astype(F32)).astype(BF16)
    return hi, lo


def _split3(x):
    hi = x.astype(BF16)
    r = x - hi.astype(F32)
    mid = r.astype(BF16)
    lo = (r - mid.astype(F32)).astype(BF16)
    return hi, mid, lo


def _sigmoid(x):
    return 0.5 * jnp.tanh(0.5 * x) + 0.5


def _silu(x):
    return x * _sigmoid(x)


def _softplus(x):
    return jnp.maximum(x, 0.0) + jnp.log(1.0 + jnp.exp(-jnp.abs(x)))


def _rms(x, g):
    return x * lax.rsqrt(jnp.mean(x * x, axis=-1, keepdims=True) + EPS) * g


def _params(sem):
    return pltpu.CompilerParams(dimension_semantics=sem, vmem_limit_bytes=VMEM_LIMIT)


def _pick(n, pref):
    t = min(n, pref)
    assert n % t == 0, (n, pref)
    return t


def _mod_kernel(c_ref, w_ref, b_ref, o_ref):
    c = c_ref[...]
    ah, al = _split2(_silu(c))
    wh, wl = _split2(w_ref[...])
    o_ref[...] = _dot(ah, wh) + _dot(ah, wl) + _dot(al, wh) + b_ref[...]


def _modulation(c, w_mod, b_mod):
    depth, d, n6 = w_mod.shape
    b = c.shape[0]
    tn = _pick(n6, 1536)
    return pl.pallas_call(
        _mod_kernel,
        out_shape=jax.ShapeDtypeStruct((depth, b, n6), F32),
        grid=(depth, n6 // tn),
        in_specs=[
            pl.BlockSpec((b, d), lambda l, j: (0, 0)),
            pl.BlockSpec((None, d, tn), lambda l, j: (l, 0, j)),
            pl.BlockSpec((None, 1, tn), lambda l, j: (l, 0, j)),
        ],
        out_specs=pl.BlockSpec((None, b, tn), lambda l, j: (l, 0, j)),
        compiler_params=_params(("parallel", "parallel")),
        name="modulation",
    )(c, w_mod, b_mod.reshape(depth, 1, n6))


def _inproj_kernel(x_ref, mod_ref, g_ref, w_ref, wdt_ref, o_ref, dt_ref):
    rows = x_ref.shape[0] // ROW_SPLIT
    for r in range(ROW_SPLIT):
        sl = slice(r * rows, (r + 1) * rows)
        h = _rms(x_ref[sl, :], g_ref[...]) * (1.0 + mod_ref[1:2, :]) + mod_ref[0:1, :]
        hb = h.astype(BF16)
        dt_ref[sl, :] = _dot(hb, wdt_ref[...])
        o_ref[sl, :] = _dot(hb, w_ref[...]).astype(BF16)


def _inproj(x2, mod_l, g, w, wdt, rows_per_batch):
    n, d = x2.shape
    tm = _pick(rows_per_batch, 512)
    bpb = rows_per_batch // tm
    return pl.pallas_call(
        _inproj_kernel,
        out_shape=(jax.ShapeDtypeStruct((n, PROJ_COLS), BF16),
                   jax.ShapeDtypeStruct((n, LANES), F32)),
        grid=(n // tm,),
        in_specs=[
            pl.BlockSpec((tm, d), lambda i: (i, 0)),
            pl.BlockSpec((None, 6, d), lambda i: (i // bpb, 0, 0)),
            pl.BlockSpec((1, d), lambda i: (0, 0)),
            pl.BlockSpec((d, PROJ_COLS), lambda i: (0, 0), **RESIDENT),
            pl.BlockSpec((d, LANES), lambda i: (0, 0), **RESIDENT),
        ],
        out_specs=(pl.BlockSpec((tm, PROJ_COLS), lambda i: (i, 0)),
                   pl.BlockSpec((tm, LANES), lambda i: (i, 0))),
        compiler_params=_params(("parallel",)),
        name="inproj",
    )(x2, mod_l, g, w, wdt)


ATTN_PAIR = 2


def _attn_kernel(q_ref, k_ref, v_ref, lq_ref, sub_ref, o_ref, qt_ref, vt_ref, m_ref, acc_ref,
                 s_ref, cmax_ref, *, tq, lam_init):
    i = pl.program_id(2)
    d = ATTN_HEAD_DIM
    n_chunks = vt_ref.shape[1]
    heads = range(ATTN_PAIR)

    def lanes(hd):
        return slice(hd * LANES, (hd + 1) * LANES)

    @pl.when(i == 0)
    def _():
        ones_row = jnp.where(lax.broadcasted_iota(I32, (BF16_ROWS, tq), 0) == 0, 1.0, 0.0).astype(BF16)
        for hd in heads:
            for cidx in range(n_chunks):
                vt = v_ref[cidx * tq:(cidx + 1) * tq, lanes(hd)].astype(F32).T.astype(BF16)
                vt_ref[hd, cidx, 0:LANES, :] = vt
                vt_ref[hd, cidx, LANES:, :] = ones_row

    for hd in heads:
        qt = (q_ref[:, lanes(hd)].astype(F32) * (d ** -0.5 * math.log2(math.e))).T
        row = lax.broadcasted_iota(I32, qt.shape, 0)
        qt_ref[hd, :, 0:tq] = jnp.where(row < d, qt, 0.0).astype(BF16)
        qt_ref[hd, :, tq:] = jnp.where(row >= d, qt, 0.0).astype(BF16)
    m_ref[...] = jnp.full(m_ref.shape, NEG_BIG, F32)
    acc_ref[...] = jnp.zeros(acc_ref.shape, F32)

    def scores(hd, j, masked):
        start = pl.multiple_of(j * tq, tq)
        s = _dot(k_ref[pl.ds(start, tq), lanes(hd)], qt_ref[hd])
        if masked:
            kk = lax.broadcasted_iota(I32, (tq, tq), 0)
            qq = lax.broadcasted_iota(I32, (tq, tq), 1)
            keep = kk <= qq
            s = jnp.where(jnp.concatenate([keep, keep], axis=1), s, NEG_BIG)
        s_ref[hd] = s
        cmax_ref[hd] = jnp.max(s, axis=0, keepdims=True)

    def accumulate(hd, j):
        m_prev = m_ref[hd]
        m_new = jnp.maximum(m_prev, cmax_ref[hd])
        alpha = jnp.exp2(m_prev - m_new)
        p = jnp.exp2(s_ref[hd] - m_new)
        acc_ref[hd] = alpha * acc_ref[hd] + _dot(vt_ref[hd, j], p.astype(BF16))
        m_ref[hd] = m_new

    scores(0, i, True)
    scores(1, i, True)
    accumulate(0, i)

    def body(j, carry):
        scores(0, j, False)
        accumulate(1, jnp.where(j == 0, i, j - 1))
        scores(1, j, False)
        accumulate(0, j)
        return carry

    lax.fori_loop(0, i, body, 0)
    accumulate(1, jnp.where(i == 0, i, i - 1))

    lq = lq_ref[...]
    lam = (jnp.exp(jnp.sum(lq[0:1, :] * lq[1:2, :], axis=-1, keepdims=True))
           - jnp.exp(jnp.sum(lq[2:3, :] * lq[3:4, :], axis=-1, keepdims=True)) + lam_init)
    for hd in heads:
        inv = 1.0 / acc_ref[hd, LANES:LANES + 1, :]
        ot = (acc_ref[hd, 0:LANES, 0:tq] * inv[:, 0:tq]
              - lam * (acc_ref[hd, 0:LANES, tq:] * inv[:, tq:]))
        ot = ot * lax.rsqrt(jnp.mean(ot * ot, axis=0, keepdims=True) + EPS)
        o_ref[:, lanes(hd)] = (ot.T * sub_ref[...] * (1.0 - lam_init)).astype(BF16)


def _attention(proj, lambda_qk, subln, batch, seq, lam_init):
    n = proj.shape[0]
    tq = _pick(seq, 512)
    nq = seq // tq
    hp = ATTN_PAIR
    wide = hp * LANES
    assert ATTN_PAIR == 2 and ATTN_HEADS % hp == 0
    return pl.pallas_call(
        functools.partial(_attn_kernel, tq=tq, lam_init=lam_init),
        out_shape=jax.ShapeDtypeStruct((n, GROUP_WIDTH), BF16),
        grid=(batch, ATTN_HEADS // hp, nq),
        in_specs=[
            pl.BlockSpec((tq, wide), lambda b, hh, i: (b * nq + i, COL_Q // wide + hh)),
            pl.BlockSpec((seq, wide), lambda b, hh, i: (b, COL_K // wide + hh)),
            pl.BlockSpec((seq, wide), lambda b, hh, i: (b, COL_V // wide + hh)),
            pl.BlockSpec((4, ATTN_HEAD_DIM), lambda b, hh, i: (0, 0)),
            pl.BlockSpec((1, LANES), lambda b, hh, i: (0, 0)),
        ],
        out_specs=pl.BlockSpec((tq, wide), lambda b, hh, i: (b * nq + i, hh)),
        scratch_shapes=[
            pltpu.VMEM((hp, LANES, 2 * tq), BF16),
            pltpu.VMEM((hp, nq, LANES + BF16_ROWS, tq), BF16),
            pltpu.VMEM((hp, 1, 2 * tq), F32),
            pltpu.VMEM((hp, LANES + BF16_ROWS, 2 * tq), F32),
            pltpu.VMEM((hp, tq, 2 * tq), F32),
            pltpu.VMEM((hp, 1, 2 * tq), F32),
        ],
        compiler_params=_params(("parallel", "parallel", "arbitrary")),
        name="diff_attention",
    )(proj, proj, proj, lambda_qk, subln)


def _causal_conv(x, carry, w, width):
    row = lax.broadcasted_iota(I32, carry.shape, 0)
    out = x * w[width - 1:width, :]
    for k in range(1, width):
        xr = pltpu.roll(x, k, 0)
        cr = pltpu.roll(carry, k, 0)
        head = jnp.where(row < k, cr, xr[0:SUBLANES, :])
        xk = jnp.concatenate([head, xr[SUBLANES:, :]], axis=0)
        out = out + xk * w[width - 1 - k:width - k, :]
    return out


def _ssd_kernel(z_ref, xbc_ref, gb_ref, gc_ref, u_ref, dt_ref, cw_ref, cb_ref, dtb_ref, alog_ref,
                dskip_ref, ng_ref, sw_ref, o_ref,
                cx_ref, cs_ref, state_ref, xs_ref, bm_ref, cm_ref, *, tc):
    L = SSD_CHUNK
    gw = GROUP_WIDTH
    ns = SSD_STATE

    @pl.when(pl.program_id(1) == 0)
    def _():
        cx_ref[...] = jnp.zeros(cx_ref.shape, F32)
        cs_ref[...] = jnp.zeros(cs_ref.shape, F32)
        state_ref[...] = jnp.zeros(state_ref.shape, F32)

    pu = gc_ref[...].astype(F32) * u_ref[...].astype(F32)
    sconv = gb_ref[...].astype(F32) * _causal_conv(pu, cs_ref[...], sw_ref[...], SC_CONV)
    cs_ref[...] = pu[tc - SUBLANES:, :]
    o_ref[:, gw:] = sconv.astype(BF16)

    xbc = xbc_ref[...].astype(F32)
    act = _silu(_causal_conv(xbc, cx_ref[...], cw_ref[...], SSD_CONV) + cb_ref[...])
    cx_ref[...] = xbc[tc - SUBLANES:, :]
    xs_ref[...] = act[:, 0:gw]
    bm_ref[...] = act[:, gw:gw + SSD_GROUPS * ns]
    cm_ref[...] = act[:, gw + SSD_GROUPS * ns:]

    lane = lax.broadcasted_iota(I32, (L, LANES), 1)
    lo = lane < SSD_HEAD_DIM
    rr = lax.broadcasted_iota(I32, (L, L), 0)
    cc = lax.broadcasted_iota(I32, (L, L), 1)
    tril = cc <= rr
    ltri = jnp.where(tril, 1.0, 0.0).astype(BF16)
    head_lane = lane < SSD_HEADS
    a_row = -jnp.exp(alog_ref[...])

    def pair_pattern(mat, h0):
        lo_b = lo[0:mat.shape[0], :]
        return jnp.where(lo_b, mat[:, h0:h0 + 1], mat[:, h0 + 1:h0 + 2])

    def chunk_body(c, carry):
        r0 = pl.multiple_of(c * L, L)
        rows = pl.ds(r0, L)
        dt = _softplus(dt_ref[rows, :] + dtb_ref[...])
        da = jnp.where(head_lane, dt * a_row, 0.0)
        d_hi, d_mid, d_lo = _split3(da)
        acum = _dot(ltri, d_hi) + _dot(ltri, d_mid) + _dot(ltri, d_lo)
        acum_t = acum.T
        a_last = acum[L - 1:L, :]
        ys = []
        for g in range(SSD_GROUPS):
            bg = bm_ref[rows, g * ns:(g + 1) * ns]
            cg = cm_ref[rows, g * ns:(g + 1) * ns].astype(BF16)
            cb = _dot_nt(cg, bg.astype(BF16))
            bg_t = bg.T.astype(BF16)
            for jp in range(SSD_HEADS // SSD_GROUPS // 2):
                j = g * (SSD_HEADS // SSD_GROUPS // 2) + jp
                h0 = 2 * j
                xs = xs_ref[rows, j * LANES:(j + 1) * LANES]
                xdt = xs * pair_pattern(dt, h0)
                y = None
                for hh, keep in ((h0, lo), (h0 + 1, jnp.logical_not(lo))):
                    seg = acum[:, hh:hh + 1] - acum_t[hh:hh + 1, :]
                    dec = jnp.exp(jnp.where(tril, seg, NEG_BIG))
                    mm = (dec * cb).astype(BF16)
                    part = _dot(mm, jnp.where(keep, xdt, 0.0).astype(BF16))
                    y = part if y is None else y + part
                st = state_ref[j]
                y = y + _dot(cg, st.astype(BF16)) * jnp.exp(pair_pattern(acum, h0))
                dte = jnp.exp(pair_pattern(a_last - acum, h0))
                contrib = _dot(bg_t, (xdt * dte).astype(BF16))
                state_ref[j] = st * jnp.exp(pair_pattern(a_last, h0)) + contrib
                ys.append(y + dskip_ref[:, j * LANES:(j + 1) * LANES] * xs)
        yv = jnp.concatenate(ys, axis=-1)
        z = z_ref[rows, :].astype(F32)
        vv = yv * _silu(z)
        gwid = gw // SSD_GROUPS
        outs = []
        for g in range(SSD_GROUPS):
            vg = vv[:, g * gwid:(g + 1) * gwid]
            outs.append(vg * lax.rsqrt(jnp.mean(vg * vg, axis=-1, keepdims=True) + EPS))
        o_ref[rows, 0:gw] = (jnp.concatenate(outs, axis=-1) * ng_ref[...]).astype(BF16)
        return carry

    lax.fori_loop(0, tc // L, chunk_body, 0)


def _ssd_sconv(proj, dt_raw, cw, cb, dtb, alog, dskip, ng, sw, batch, seq):
    n = proj.shape[0]
    tc = _pick(seq, 512)
    nt = seq // tc
    row = lambda b, t: b * nt + t
    const = lambda b, t: (0, 0)
    return pl.pallas_call(
        functools.partial(_ssd_kernel, tc=tc),
        out_shape=jax.ShapeDtypeStruct((n, 2 * GROUP_WIDTH), BF16),
        grid=(batch, nt),
        in_specs=[
            pl.BlockSpec((tc, GROUP_WIDTH), lambda b, t: (row(b, t), COL_Z // GROUP_WIDTH)),
            pl.BlockSpec((tc, SSD_XBC), lambda b, t: (row(b, t), COL_XBC // SSD_XBC)),
            pl.BlockSpec((tc, GROUP_WIDTH), lambda b, t: (row(b, t), COL_GB // GROUP_WIDTH)),
            pl.BlockSpec((tc, GROUP_WIDTH), lambda b, t: (row(b, t), COL_GC // GROUP_WIDTH)),
            pl.BlockSpec((tc, GROUP_WIDTH), lambda b, t: (row(b, t), COL_U // GROUP_WIDTH)),
            pl.BlockSpec((tc, LANES), lambda b, t: (row(b, t), 0)),
            pl.BlockSpec((SSD_CONV, SSD_XBC), const),
            pl.BlockSpec((1, SSD_XBC), const),
            pl.BlockSpec((1, LANES), const),
            pl.BlockSpec((1, LANES), const),
            pl.BlockSpec((1, GROUP_WIDTH), const),
            pl.BlockSpec((1, GROUP_WIDTH), const),
            pl.BlockSpec((SC_CONV, GROUP_WIDTH), const),
        ],
        out_specs=pl.BlockSpec((tc, 2 * GROUP_WIDTH), lambda b, t: (row(b, t), 0)),
        scratch_shapes=[
            pltpu.VMEM((SUBLANES, SSD_XBC), F32),
            pltpu.VMEM((SUBLANES, GROUP_WIDTH), F32),
            pltpu.VMEM((SSD_HEADS // 2, SSD_STATE, LANES), F32),
            pltpu.VMEM((tc, GROUP_WIDTH), F32),
            pltpu.VMEM((tc, SSD_GROUPS * SSD_STATE), F32),
            pltpu.VMEM((tc, SSD_GROUPS * SSD_STATE), F32),
        ],
        compiler_params=_params(("parallel", "arbitrary")),
        name="ssd_sconv",
    )(proj, proj, proj, proj, proj, dt_raw, cw, cb, dtb, alog, dskip, ng, sw)


def _outproj_kernel(a_ref, sc_ref, w_ref, x_ref, mod_ref, g_ref, o_ref):
    gw = GROUP_WIDTH
    rows = x_ref.shape[0] // ROW_SPLIT
    for r in range(ROW_SPLIT):
        sl = slice(r * rows, (r + 1) * rows)
        y = _dot(a_ref[sl, :], w_ref[0:gw, :]) + _dot(sc_ref[sl, :], w_ref[gw:, :])
        o_ref[sl, :] = x_ref[sl, :] + mod_ref[2:3, :] * _rms(y, g_ref[...])


def _outproj(attn, sc, w, x2, mod_l, g, rows_per_batch):
    n, d = x2.shape
    tm = _pick(rows_per_batch, 512)
    bpb = rows_per_batch // tm
    return pl.pallas_call(
        _outproj_kernel,
        out_shape=jax.ShapeDtypeStruct((n, d), F32),
        grid=(n // tm,),
        in_specs=[
            pl.BlockSpec((tm, GROUP_WIDTH), lambda i: (i, 0)),
            pl.BlockSpec((tm, 2 * GROUP_WIDTH), lambda i: (i, 0)),
            pl.BlockSpec((3 * GROUP_WIDTH, d), lambda i: (0, 0), **RESIDENT),
            pl.BlockSpec((tm, d), lambda i: (i, 0)),
            pl.BlockSpec((None, 6, d), lambda i: (i // bpb, 0, 0)),
            pl.BlockSpec((1, d), lambda i: (0, 0)),
        ],
        out_specs=pl.BlockSpec((tm, d), lambda i: (i, 0)),
        compiler_params=_params(("parallel",)),
        name="outproj",
    )(attn, sc, w, x2, mod_l, g)


def _ffn_kernel(x_ref, mod_ref, gpre_ref, gpost_ref, w1_ref, w3_ref, w2_ref, o_ref):
    rows = x_ref.shape[0] // ROW_SPLIT
    for r in range(ROW_SPLIT):
        sl = slice(r * rows, (r + 1) * rows)
        x = x_ref[sl, :]
        h = (_rms(x, gpre_ref[...]) * (1.0 + mod_ref[4:5, :]) + mod_ref[3:4, :]).astype(BF16)
        t = (_silu(_dot(h, w1_ref[...])) * _dot(h, w3_ref[...])).astype(BF16)
        y = _dot(t, w2_ref[...])
        o_ref[sl, :] = x + mod_ref[5:6, :] * _rms(y, gpost_ref[...])


def _ffn(x2, mod_l, gpre, gpost, w1, w3, w2, rows_per_batch):
    n, d = x2.shape
    dff = w1.shape[1]
    tm = _pick(rows_per_batch, 512)
    bpb = rows_per_batch // tm
    return pl.pallas_call(
        _ffn_kernel,
        out_shape=jax.ShapeDtypeStruct((n, d), F32),
        grid=(n // tm,),
        in_specs=[
            pl.BlockSpec((tm, d), lambda i: (i, 0)),
            pl.BlockSpec((None, 6, d), lambda i: (i // bpb, 0, 0)),
            pl.BlockSpec((1, d), lambda i: (0, 0)),
            pl.BlockSpec((1, d), lambda i: (0, 0)),
            pl.BlockSpec((d, dff), lambda i: (0, 0), **RESIDENT),
            pl.BlockSpec((d, dff), lambda i: (0, 0), **RESIDENT),
            pl.BlockSpec((dff, d), lambda i: (0, 0), **RESIDENT),
        ],
        out_specs=pl.BlockSpec((tm, d), lambda i: (i, 0)),
        compiler_params=_params(("parallel",)),
        name="dense_ffn",
    )(x2, mod_l, gpre, gpost, w1, w3, w2)


MOE_TM = 512
DSP_TM = 256
DSP_WIN = 256
DSP_DEPTH = 4
CMB_T = 512
CMB_WIN = 256
CMB_MAX_SUB = (CMB_T + BF16_ROWS + CMB_WIN - 1) // CMB_WIN


def _router_kernel(x_ref, mod_ref, g_ref, wr_ref, h_ref, route_ref, wcum_ref, run_ref):
    @pl.when(pl.program_id(0) == 0)
    def _():
        run_ref[...] = jnp.zeros(run_ref.shape, F32)

    h = _rms(x_ref[...], g_ref[...]) * (1.0 + mod_ref[4:5, :]) + mod_ref[3:4, :]
    hh, hl = _split2(h)
    h_ref[...] = hh
    a = _dot(hh, wr_ref[...])
    b = _dot(hl, wr_ref[...])
    ne = N_EXPERTS
    logits = a + b + pltpu.roll(a, LANES - ne, 1)
    lane = lax.broadcasted_iota(I32, logits.shape, 1)
    valid = lane < ne
    l1 = jnp.max(jnp.where(valid, logits, NEG_BIG), axis=-1, keepdims=True)
    e1 = jnp.min(jnp.where(valid & (logits == l1), lane, LANES), axis=-1, keepdims=True)
    rest = valid & (lane != e1)
    l2 = jnp.max(jnp.where(rest, logits, NEG_BIG), axis=-1, keepdims=True)
    e2 = jnp.min(jnp.where(rest & (logits == l2), lane, LANES), axis=-1, keepdims=True)
    w = jnp.exp(l2 - l1)
    g1 = 1.0 / (1.0 + w)
    g2 = w / (1.0 + w)

    tm = logits.shape[0]
    sel = jnp.where(lane == e1, 1.0, jnp.where(lane == e2, 1.0, 0.0))
    rr = lax.broadcasted_iota(I32, (tm, tm), 0)
    cc = lax.broadcasted_iota(I32, (tm, tm), 1)
    before = jnp.where(cc < rr, 1.0, 0.0).astype(BF16)
    excl = _dot(before, sel.astype(BF16)) + run_ref[...]
    rank1 = jnp.sum(jnp.where(lane == e1, excl, 0.0), axis=-1, keepdims=True)
    rank2 = jnp.sum(jnp.where(lane == e2, excl, 0.0), axis=-1, keepdims=True)
    total = excl + sel
    for wdx in range(tm // DSP_WIN):
        wcum_ref[wdx:wdx + 1, :] = total[(wdx + 1) * DSP_WIN - 1:(wdx + 1) * DSP_WIN, :]
    run_ref[...] = total[tm - 1:tm, :]

    lane8 = lax.broadcasted_iota(I32, route_ref.shape, 1)
    cols = (e1.astype(F32), e2.astype(F32), g1, g2, rank1, rank2)
    out = jnp.zeros(route_ref.shape, F32)
    for idx, col in enumerate(cols):
        out = jnp.where(lane8 == idx, col, out)
    route_ref[...] = out


def _router(x2, mod_l, g, wr_packed, rows_per_batch):
    n, d = x2.shape
    tm = _pick(rows_per_batch, 512)
    assert tm % DSP_WIN == 0
    bpb = rows_per_batch // tm
    return pl.pallas_call(
        _router_kernel,
        out_shape=(jax.ShapeDtypeStruct((n, d), BF16), jax.ShapeDtypeStruct((n, SUBLANES), F32),
                   jax.ShapeDtypeStruct((n // tm, tm // DSP_WIN, LANES), F32)),
        grid=(n // tm,),
        in_specs=[
            pl.BlockSpec((tm, d), lambda i: (i, 0)),
            pl.BlockSpec((None, 6, d), lambda i: (i // bpb, 0, 0)),
            pl.BlockSpec((1, d), lambda i: (0, 0)),
            pl.BlockSpec((d, LANES), lambda i: (0, 0)),
        ],
        out_specs=(pl.BlockSpec((tm, d), lambda i: (i, 0)),
                   pl.BlockSpec((tm, SUBLANES), lambda i: (i, 0)),
                   pl.BlockSpec((None, tm // DSP_WIN, LANES), lambda i: (i, 0, 0))),
        scratch_shapes=[pltpu.VMEM((1, LANES), F32)],
        compiler_params=_params(("arbitrary",)),
        name="moe_router",
    )(x2, mod_l, g, wr_packed)


def _dispatch_kernel(meta_ref, wc_ref, h_hbm, pos_ref, gt_ref, xb_ref, gate_ref,
                     buf_ref, sem_ref, acc_ref, gacc_ref, st_ref):
    b = pl.program_id(0)
    nb = pl.num_programs(0)
    ne = N_EXPERTS
    tm, win = DSP_TM, DSP_WIN
    nslot = DSP_DEPTH + 1
    nwt = pos_ref.shape[0]

    def copy(w, slot):
        return pltpu.make_async_copy(h_hbm.at[pl.ds(pl.multiple_of(w * win, win), win), :],
                                     buf_ref.at[slot], sem_ref.at[slot])

    def step_range(s):
        p = s * tm
        e = jnp.zeros((), I32)
        for j in range(ne - 1):
            e = e + (p >= meta_ref[2 * ne + j]).astype(I32)
        r0 = p - meta_ref[e]
        cnt = meta_ref[ne + e]
        r1 = jnp.minimum(r0 + tm, cnt)

        def first_window_reaching(target):
            def halve(_, c):
                lo, hi = c
                mid = (lo + hi) // 2
                less = wc_ref[e * nwt + mid] < target
                return jnp.where(less, mid + 1, lo), jnp.where(less, hi, mid)

            return lax.fori_loop(0, max(nwt - 1, 1).bit_length(), halve,
                                 (jnp.zeros((), I32), jnp.full((), nwt - 1, I32)))[0]

        w_first = first_window_reaching(r0 + 1)
        w_last = first_window_reaching(r1)
        has = r0 < cnt
        return jnp.where(has, w_first, 0), jnp.where(has, w_last - w_first + 1, 0)

    def produce():
        def exhausted(c):
            return (c[0] < nb) & (c[1] >= c[3])

        def next_step(c):
            s = c[0] + 1
            w_first, count = step_range(jnp.minimum(s, nb - 1))
            return s, jnp.zeros_like(c[1]), w_first, count

        ps, pk, pw, pn = lax.while_loop(exhausted, next_step,
                                        (st_ref[0], st_ref[1], st_ref[4], st_ref[5]))
        st_ref[0] = ps
        st_ref[4] = pw
        st_ref[5] = pn

        @pl.when(ps < nb)
        def _():
            issued = st_ref[2]
            copy(pw + pk, issued % nslot).start()
            st_ref[1] = pk + 1
            st_ref[2] = issued + 1

        @pl.when(ps >= nb)
        def _():
            st_ref[1] = pk

    w0, nw = step_range(b)

    @pl.when(b == 0)
    def _():
        for idx in range(4):
            st_ref[idx] = 0
        st_ref[4] = w0
        st_ref[5] = nw
        for _ in range(DSP_DEPTH):
            produce()

    g0 = st_ref[3]
    acc_ref[...] = jnp.zeros(acc_ref.shape, F32)
    gacc_ref[...] = jnp.zeros(gacc_ref.shape, F32)
    slot_id = b * tm + lax.broadcasted_iota(I32, (tm, win), 0)

    def body(k, carry):
        slot = (g0 + k) % nslot
        copy(w0 + k, slot).wait()
        produce()
        pos = pos_ref[w0 + k]
        gt = gt_ref[w0 + k]
        eq0 = pos[0:1, :] == slot_id
        eq1 = pos[1:2, :] == slot_id
        onehot = jnp.where(eq0, 1.0, jnp.where(eq1, 1.0, 0.0)).astype(BF16)
        acc_ref[...] += _dot(onehot, buf_ref[slot])
        gsel = jnp.where(eq0, gt[0:1, :], 0.0) + jnp.where(eq1, gt[1:2, :], 0.0)
        gacc_ref[...] += jnp.sum(gsel, axis=-1, keepdims=True)
        return carry

    lax.fori_loop(0, nw, body, 0)
    st_ref[3] = g0 + nw
    xb_ref[...] = acc_ref[...].astype(BF16)
    gate_ref[...] = gacc_ref[...]


def _dispatch(meta, wcum_flat, h, pos_w, gate_w, n_steps):
    n, d = h.shape
    tm, win = DSP_TM, DSP_WIN
    nwt = n // win
    return pl.pallas_call(
        _dispatch_kernel,
        out_shape=(jax.ShapeDtypeStruct((n_steps * tm, d), BF16),
                   jax.ShapeDtypeStruct((n_steps * tm, 1), F32)),
        grid_spec=pltpu.PrefetchScalarGridSpec(
            num_scalar_prefetch=2,
            grid=(n_steps,),
            in_specs=[
                pl.BlockSpec(memory_space=pl.ANY),
                pl.BlockSpec((nwt, 2, win), lambda b, w0, nw: (0, 0, 0)),
                pl.BlockSpec((nwt, 2, win), lambda b, w0, nw: (0, 0, 0)),
            ],
            out_specs=(pl.BlockSpec((tm, d), lambda b, w0, nw: (b, 0)),
                       pl.BlockSpec((tm, 1), lambda b, w0, nw: (b, 0))),
            scratch_shapes=[
                pltpu.VMEM((DSP_DEPTH + 1, win, d), BF16),
                pltpu.SemaphoreType.DMA((DSP_DEPTH + 1,)),
                pltpu.VMEM((tm, d), F32),
                pltpu.VMEM((tm, 1), F32),
                pltpu.SMEM((6,), I32),
            ],
        ),
        compiler_params=_params(("arbitrary",)),
        name="moe_dispatch",
    )(meta, wcum_flat, h, pos_w, gate_w)


def _expert_kernel(be_ref, used_ref, xb_ref, gate_ref, w1_ref, w3_ref, w2_ref, yb_ref, acc_ref):
    b = pl.program_id(0)
    j = pl.program_id(1)
    last = pl.num_programs(1) - 1
    live = b < used_ref[0]

    @pl.when((b == 0) & (j == 0))
    def _():
        acc_ref[...] = jnp.zeros(acc_ref.shape, F32)

    @pl.when(live)
    def _():
        rows = xb_ref.shape[0] // ROW_SPLIT
        for r in range(ROW_SPLIT):
            sl = slice(r * rows, (r + 1) * rows)
            xb = xb_ref[sl, :]
            t = (_silu(_dot(xb, w1_ref[...])) * _dot(xb, w3_ref[...])).astype(BF16)
            part = _dot(t, w2_ref[...])
            acc_ref[sl, :] = jnp.where(j == 0, 0.0, acc_ref[sl, :]) + part

        @pl.when(j == last)
        def _():
            yb_ref[...] = (acc_ref[...] * gate_ref[...]).astype(BF16)

    @pl.when(jnp.logical_not(live) & (j == last))
    def _():
        yb_ref[...] = jnp.zeros(yb_ref.shape, BF16)


def _experts(block_expert, used, xb, gate_buf, w1, w3, w2, n_blocks):
    d = xb.shape[1]
    dff = w1.shape[2]
    tm = MOE_TM
    tf = dff // 2 if (dff // 2) % LANES == 0 else dff
    nj = dff // tf

    def jj(b, j, be, used):
        return jnp.where(b < used[0], j, nj - 1)

    return pl.pallas_call(
        _expert_kernel,
        out_shape=jax.ShapeDtypeStruct((n_blocks * tm, d), BF16),
        grid_spec=pltpu.PrefetchScalarGridSpec(
            num_scalar_prefetch=2,
            grid=(n_blocks, nj),
            in_specs=[
                pl.BlockSpec((tm, d), lambda b, j, be, used: (b, 0)),
                pl.BlockSpec((tm, 1), lambda b, j, be, used: (b, 0)),
                pl.BlockSpec((None, d, tf), lambda b, j, be, used: (be[b], 0, jj(b, j, be, used))),
                pl.BlockSpec((None, d, tf), lambda b, j, be, used: (be[b], 0, jj(b, j, be, used))),
                pl.BlockSpec((None, tf, d), lambda b, j, be, used: (be[b], jj(b, j, be, used), 0)),
            ],
            out_specs=pl.BlockSpec((tm, d), lambda b, j, be, used: (b, 0)),
            scratch_shapes=[pltpu.VMEM((tm, d), F32)],
        ),
        compiler_params=_params(("arbitrary", "arbitrary")),
        name="moe_experts",
    )(block_expert, used, xb, gate_buf, w1, w3, w2)


def _combine_kernel(ws_ref, nsub_ref, yb_hbm, pos_ref, x_ref, mod_ref, g_ref, o_ref,
                    buf_ref, sem_ref, xbuf_ref, xsem_ref, y_ref):
    i = pl.program_id(0)
    n_steps = pl.num_programs(0)
    ne = N_EXPERTS
    t, cw = CMB_T, CMB_WIN

    def start_row(step, e, sub):
        return pl.multiple_of(ws_ref[step * ne + e] + sub * cw, BF16_ROWS)

    def copy(step, e, slot):
        return pltpu.make_async_copy(yb_hbm.at[pl.ds(start_row(step, e, 0), cw), :],
                                     buf_ref.at[slot, e], sem_ref.at[slot, e])

    @pl.when(i == 0)
    def _():
        for e in range(ne):
            copy(0, e, 0).start()

    slot = i % 2

    @pl.when(i + 1 < n_steps)
    def _():
        for e in range(ne):
            copy(i + 1, e, 1 - slot).start()

    pe = pos_ref[...]
    pos = pe[:, 0:TOP_K]
    top = pe[:, TOP_K:2 * TOP_K]
    col = lax.broadcasted_iota(I32, (t, cw), 1)

    def onehot(rel):
        return jnp.where(rel[:, 0:1] == col, 1.0, jnp.where(rel[:, 1:2] == col, 1.0, 0.0)).astype(BF16)

    rels = []
    for e in range(ne):
        copy(i, e, slot).wait()
        rels.append(jnp.where(top == e, pos - ws_ref[i * ne + e], -1))
    picks = jnp.concatenate([onehot(rel) for rel in rels], axis=1)
    y_ref[...] = _dot(picks, buf_ref[slot].reshape(ne * cw, buf_ref.shape[-1]))

    for e in range(ne):
        nsub = nsub_ref[i * ne + e]
        for sub in range(1, CMB_MAX_SUB):
            @pl.when(nsub > sub)
            def _():
                extra = pltpu.make_async_copy(yb_hbm.at[pl.ds(start_row(i, e, sub), cw), :],
                                              xbuf_ref, xsem_ref)
                extra.start()
                extra.wait()
                y_ref[...] += _dot(onehot(rels[e] - sub * cw), xbuf_ref[...])

    o_ref[...] = x_ref[...] + mod_ref[5:6, :] * _rms(y_ref[...], g_ref[...])


def _combine(win_start, nsub, yb, pos, x2, mod_l, g, rows_per_batch):
    n, d = x2.shape
    t = CMB_T
    bpb = rows_per_batch // t
    return pl.pallas_call(
        _combine_kernel,
        out_shape=jax.ShapeDtypeStruct((n, d), F32),
        grid_spec=pltpu.PrefetchScalarGridSpec(
            num_scalar_prefetch=2,
            grid=(n // t,),
            in_specs=[
                pl.BlockSpec(memory_space=pl.ANY),
                pl.BlockSpec((t, 2 * TOP_K), lambda i, ws, ns: (i, 0)),
                pl.BlockSpec((t, d), lambda i, ws, ns: (i, 0)),
                pl.BlockSpec((None, 6, d), lambda i, ws, ns: (i // bpb, 0, 0)),
                pl.BlockSpec((1, d), lambda i, ws, ns: (0, 0)),
            ],
            out_specs=pl.BlockSpec((t, d), lambda i, ws, ns: (i, 0)),
            scratch_shapes=[
                pltpu.VMEM((2, N_EXPERTS, CMB_WIN, d), BF16),
                pltpu.SemaphoreType.DMA((2, N_EXPERTS)),
                pltpu.VMEM((CMB_WIN, d), BF16),
                pltpu.SemaphoreType.DMA(()),
                pltpu.VMEM((t, d), F32),
            ],
        ),
        compiler_params=_params(("arbitrary",)),
        name="moe_combine",
    )(win_start, nsub, yb, pos, x2, mod_l, g)


def _moe(x2, mod_l, gpre, gpost, w_router, w1, w3, w2, rows_per_batch):
    n, d = x2.shape
    ne, tm, dtm, win, t = N_EXPERTS, MOE_TM, DSP_TM, DSP_WIN, CMB_T
    wr_hi, wr_lo = _split2(w_router)
    wr_packed = jnp.zeros((d, LANES), BF16).at[:, 0:ne].set(wr_hi).at[:, ne:2 * ne].set(wr_lo)
    h, route, wcum = _router(x2, mod_l, gpre, wr_packed, rows_per_batch)

    top = route[:, 0:TOP_K].astype(I32)
    gates = route[:, TOP_K:2 * TOP_K]
    rank = route[:, 2 * TOP_K:3 * TOP_K].astype(I32)
    wc = wcum.reshape(n // win, LANES)[:, 0:ne].astype(I32)
    counts = wc[-1]
    padded = (counts + tm - 1) // tm * tm
    pend = jnp.cumsum(padded)
    pstart = pend - padded
    experts = jnp.arange(ne, dtype=I32)
    pos = rank + jnp.sum(jnp.where(top[:, :, None] == experts, pstart, 0), axis=-1)

    n_blocks = (n * TOP_K) // tm + ne + 1
    blk = jnp.arange(n_blocks, dtype=I32) * tm
    block_expert = jnp.minimum(jnp.sum((pend[None, :] <= blk[:, None]).astype(I32), axis=1), ne - 1)
    used = (pend[-1] // tm).astype(I32).reshape(1)

    n_steps = n_blocks * (tm // dtm)
    meta = jnp.concatenate([pstart, counts, pend]).astype(I32)
    pos_w = pos.T.reshape(TOP_K, n // win, win).transpose(1, 0, 2)
    gate_w = gates.T.reshape(TOP_K, n // win, win).transpose(1, 0, 2)
    xb, gate_buf = _dispatch(meta, wc.T.reshape(-1), h, pos_w, gate_w, n_steps)
    yb = _experts(block_expert, used, xb, gate_buf, w1, w3, w2, n_blocks)

    at_block_end = wc[t // win - 1::t // win]
    before_block = jnp.concatenate([jnp.zeros((1, ne), I32), at_block_end[:-1]], axis=0)
    base_b = pstart[None, :] + before_block
    end_b = pstart[None, :] + at_block_end
    win_start = base_b // BF16_ROWS * BF16_ROWS
    nsub = jnp.where(end_b > base_b, (end_b - win_start + CMB_WIN - 1) // CMB_WIN, 0)
    return _combine(win_start.astype(I32).reshape(-1), nsub.astype(I32).reshape(-1), yb,
                    jnp.concatenate([pos, top], axis=1), x2, mod_l, gpost, rows_per_batch)


def _reorder_w_in(w):
    main = jnp.concatenate([w[:, 0:3072], w[:, 3080:4616]], axis=1).astype(BF16)
    wdt = jnp.zeros((w.shape[0], LANES), BF16).at[:, 0:SSD_HEADS].set(w[:, 3072:3080].astype(BF16))
    return main, wdt


def _pad_row(v, width=LANES):
    return jnp.zeros((1, width), F32).at[0, 0:v.shape[0]].set(v)


def kernel(x, c, w_mod, b_mod, norm_mix_pre, norm_mix_post, norm_ffn_pre, norm_ffn_post, w_in, w_out, lambda_qk, attn_subln, ssd_conv_w, ssd_conv_b, ssd_dt_bias, ssd_a_log, ssd_d, ssd_norm, sconv_w, ffn_w1, ffn_w3, ffn_w2, moe_router, moe_w1, moe_w3, moe_w2):
    batch, seq, d = x.shape
    depth = w_mod.shape[0]
    n = batch * seq
    assert seq % SSD_CHUNK == 0 and w_in.shape[2] == PROJ_COLS + SSD_HEADS
    mod = _modulation(c, w_mod, b_mod).reshape(depth, batch, 6, d)
    x2 = x.reshape(n, d)
    for i in range(depth):
        mod_l = mod[i]
        w_main, w_dt = _reorder_w_in(w_in[i])
        proj, dt_raw = _inproj(x2, mod_l, norm_mix_pre[i][None, :], w_main, w_dt, seq)
        lam_init = 0.8 - 0.6 * math.exp(-0.3 * i)
        attn = _attention(proj, lambda_qk[i], attn_subln[i][None, :], batch, seq, lam_init)
        sc = _ssd_sconv(proj, dt_raw, ssd_conv_w[i], ssd_conv_b[i][None, :], _pad_row(ssd_dt_bias[i]),
                        _pad_row(ssd_a_log[i]), jnp.repeat(ssd_d[i], SSD_HEAD_DIM)[None, :],
                        ssd_norm[i][None, :], sconv_w[i], batch, seq)
        x2 = _outproj(attn, sc, w_out[i].astype(BF16), x2, mod_l, norm_mix_post[i][None, :], seq)
        if i % 2 == 0:
            x2 = _ffn(x2, mod_l, norm_ffn_pre[i][None, :], norm_ffn_post[i][None, :],
                      ffn_w1[i // 2].astype(BF16), ffn_w3[i // 2].astype(BF16), ffn_w2[i // 2].astype(BF16), seq)
        else:
            x2 = _moe(x2, mod_l, norm_ffn_pre[i][None, :], norm_ffn_post[i][None, :], moe_router[i // 2],
                      moe_w1[i // 2].astype(BF16), moe_w3[i // 2].astype(BF16), moe_w2[i // 2].astype(BF16), seq)
    return x2.reshape(batch, seq, d)
```

```python
import functools
import math

import jax
import jax.numpy as jnp
from jax import lax
from jax.experimental import pallas as pl
from jax.experimental.pallas import tpu as pltpu

F32 = jnp.float32
BF16 = jnp.bfloat16
I32 = jnp.int32

EPS = 1e-6
GROUP_WIDTH = 512
ATTN_HEADS = 4
ATTN_HEAD_DIM = 64
SSD_HEAD_DIM = 64
SSD_HEADS = 8
SSD_GROUPS = 2
SSD_STATE = 128
SSD_CONV = 4
SSD_CHUNK = 128
SSD_XBC = GROUP_WIDTH + 2 * SSD_GROUPS * SSD_STATE
SC_CONV = 3
N_EXPERTS = 8
TOP_K = 2

LANES = 128
SUBLANES = 8
BF16_ROWS = 16
VMEM_LIMIT = 48 * 1024 * 1024
NEG_BIG = -1e30
ROW_SPLIT = 2
RESIDENT = dict(pipeline_mode=pl.Buffered(1))

COL_Q, COL_K, COL_V, COL_Z, COL_XBC, COL_GB, COL_GC, COL_U = 0, 512, 1024, 1536, 2048, 3072, 3584, 4096
PROJ_COLS = 4608

NT_DIMS = (((1,), (1,)), ((), ()))


def _dot(a, b):
    return jnp.dot(a, b, preferred_element_type=F32)


def _dot_nt(a, b):
    return lax.dot_general(a, b, NT_DIMS, preferred_element_type=F32)


def _split2(x):
    hi = x.astype(BF16)
    lo = (x - hi.astype(F32)).astype(BF16)
    return hi, lo


def _split3(x):
    hi = x.astype(BF16)
    r = x - hi.astype(F32)
    mid = r.astype(BF16)
    lo = (r - mid.astype(F32)).astype(BF16)
    return hi, mid, lo


def _sigmoid(x):
    return 0.5 * jnp.tanh(0.5 * x) + 0.5


def _silu(x):
    return x * _sigmoid(x)


def _softplus(x):
    return jnp.maximum(x, 0.0) + jnp.log(1.0 + jnp.exp(-jnp.abs(x)))


def _rms(x, g):
    return x * lax.rsqrt(jnp.mean(x * x, axis=-1, keepdims=True) + EPS) * g


def _params(sem):
    return pltpu.CompilerParams(dimension_semantics=sem, vmem_limit_bytes=VMEM_LIMIT)


def _pick(n, pref):
    t = min(n, pref)
    assert n % t == 0, (n, pref)
    return t


def _mod_kernel(c_ref, w_ref, b_ref, o_ref):
    c = c_ref[...]
    ah, al = _split2(_silu(c))
    wh, wl = _split2(w_ref[...])
    o_ref[...] = _dot(ah, wh) + _dot(ah, wl) + _dot(al, wh) + b_ref[...]


def _modulation(c, w_mod, b_mod):
    depth, d, n6 = w_mod.shape
    b = c.shape[0]
    tn = _pick(n6, 1536)
    return pl.pallas_call(
        _mod_kernel,
        out_shape=jax.ShapeDtypeStruct((depth, b, n6), F32),
        grid=(depth, n6 // tn),
        in_specs=[
            pl.BlockSpec((b, d), lambda l, j: (0, 0)),
            pl.BlockSpec((None, d, tn), lambda l, j: (l, 0, j)),
            pl.BlockSpec((None, 1, tn), lambda l, j: (l, 0, j)),
        ],
        out_specs=pl.BlockSpec((None, b, tn), lambda l, j: (l, 0, j)),
        compiler_params=_params(("parallel", "parallel")),
        name="modulation",
    )(c, w_mod, b_mod.reshape(depth, 1, n6))


def _inproj_kernel(x_ref, mod_ref, g_ref, w_ref, wdt_ref, o_ref, dt_ref):
    rows = x_ref.shape[0] // ROW_SPLIT
    for r in range(ROW_SPLIT):
        sl = slice(r * rows, (r + 1) * rows)
        h = _rms(x_ref[sl, :], g_ref[...]) * (1.0 + mod_ref[1:2, :]) + mod_ref[0:1, :]
        hb = h.astype(BF16)
        dt_ref[sl, :] = _dot(hb, wdt_ref[...])
        o_ref[sl, :] = _dot(hb, w_ref[...]).astype(BF16)


def _inproj(x2, mod_l, g, w, wdt, rows_per_batch):
    n, d = x2.shape
    tm = _pick(rows_per_batch, 512)
    bpb = rows_per_batch // tm
    return pl.pallas_call(
        _inproj_kernel,
        out_shape=(jax.ShapeDtypeStruct((n, PROJ_COLS), BF16),
                   jax.ShapeDtypeStruct((n, LANES), F32)),
        grid=(n // tm,),
        in_specs=[
            pl.BlockSpec((tm, d), lambda i: (i, 0)),
            pl.BlockSpec((None, 6, d), lambda i: (i // bpb, 0, 0)),
            pl.BlockSpec((1, d), lambda i: (0, 0)),
            pl.BlockSpec((d, PROJ_COLS), lambda i: (0, 0), **RESIDENT),
            pl.BlockSpec((d, LANES), lambda i: (0, 0), **RESIDENT),
        ],
        out_specs=(pl.BlockSpec((tm, PROJ_COLS), lambda i: (i, 0)),
                   pl.BlockSpec((tm, LANES), lambda i: (i, 0))),
        compiler_params=_params(("parallel",)),
        name="inproj",
    )(x2, mod_l, g, w, wdt)


ATTN_PAIR = 2


def _attn_kernel(q_ref, k_ref, v_ref, lq_ref, sub_ref, o_ref, qt_ref, vt_ref, m_ref, acc_ref,
                 s_ref, cmax_ref, *, tq, lam_init):
    i = pl.program_id(2)
    d = ATTN_HEAD_DIM
    n_chunks = vt_ref.shape[1]
    heads = range(ATTN_PAIR)

    def lanes(hd):
        return slice(hd * LANES, (hd + 1) * LANES)

    @pl.when(i == 0)
    def _():
        ones_row = jnp.where(lax.broadcasted_iota(I32, (BF16_ROWS, tq), 0) == 0, 1.0, 0.0).astype(BF16)
        for hd in heads:
            for cidx in range(n_chunks):
                vt = v_ref[cidx * tq:(cidx + 1) * tq, lanes(hd)].astype(F32).T.astype(BF16)
                vt_ref[hd, cidx, 0:LANES, :] = vt
                vt_ref[hd, cidx, LANES:, :] = ones_row

    for hd in heads:
        qt = (q_ref[:, lanes(hd)].astype(F32) * (d ** -0.5 * math.log2(math.e))).T
        row = lax.broadcasted_iota(I32, qt.shape, 0)
        qt_ref[hd, :, 0:tq] = jnp.where(row < d, qt, 0.0).astype(BF16)
        qt_ref[hd, :, tq:] = jnp.where(row >= d, qt, 0.0).astype(BF16)
    m_ref[...] = jnp.full(m_ref.shape, NEG_BIG, F32)
    acc_ref[...] = jnp.zeros(acc_ref.shape, F32)

    def scores(hd, j, masked):
        start = pl.multiple_of(j * tq, tq)
        s = _dot(k_ref[pl.ds(start, tq), lanes(hd)], qt_ref[hd])
        if masked:
            kk = lax.broadcasted_iota(I32, (tq, tq), 0)
            qq = lax.broadcasted_iota(I32, (tq, tq), 1)
            keep = kk <= qq
            s = jnp.where(jnp.concatenate([keep, keep], axis=1), s, NEG_BIG)
        s_ref[hd] = s
        cmax_ref[hd] = jnp.max(s, axis=0, keepdims=True)

    def accumulate(hd, j):
        m_prev = m_ref[hd]
        m_new = jnp.maximum(m_prev, cmax_ref[hd])
        alpha = jnp.exp2(m_prev - m_new)
        p = jnp.exp2(s_ref[hd] - m_new)
        acc_ref[hd] = alpha * acc_ref[hd] + _dot(vt_ref[hd, j], p.astype(BF16))
        m_ref[hd] = m_new

    scores(0, i, True)
    scores(1, i, True)
    accumulate(0, i)

    def chunk_step(j):
        scores(0, j, False)
        accumulate(1, jnp.where(j == 0, i, j - 1))
        scores(1, j, False)
        accumulate(0, j)

    def body(u, carry):
        chunk_step(2 * u)
        chunk_step(2 * u + 1)
        return carry

    lax.fori_loop(0, i // 2, body, 0)

    @pl.when(i % 2 == 1)
    def _():
        chunk_step(i - 1)

    accumulate(1, jnp.where(i == 0, i, i - 1))

    lq = lq_ref[...]
    lam = (jnp.exp(jnp.sum(lq[0:1, :] * lq[1:2, :], axis=-1, keepdims=True))
           - jnp.exp(jnp.sum(lq[2:3, :] * lq[3:4, :], axis=-1, keepdims=True)) + lam_init)
    for hd in heads:
        inv = 1.0 / acc_ref[hd, LANES:LANES + 1, :]
        ot = (acc_ref[hd, 0:LANES, 0:tq] * inv[:, 0:tq]
              - lam * (acc_ref[hd, 0:LANES, tq:] * inv[:, tq:]))
        ot = ot * lax.rsqrt(jnp.mean(ot * ot, axis=0, keepdims=True) + EPS)
        o_ref[:, lanes(hd)] = (ot.T * sub_ref[...] * (1.0 - lam_init)).astype(BF16)


def _attention(proj, lambda_qk, subln, batch, seq, lam_init):
    n = proj.shape[0]
    tq = _pick(seq, 512)
    nq = seq // tq
    hp = ATTN_PAIR
    wide = hp * LANES
    assert ATTN_PAIR == 2 and ATTN_HEADS % hp == 0
    return pl.pallas_call(
        functools.partial(_attn_kernel, tq=tq, lam_init=lam_init),
        out_shape=jax.ShapeDtypeStruct((n, GROUP_WIDTH), BF16),
        grid=(batch, ATTN_HEADS // hp, nq),
        in_specs=[
            pl.BlockSpec((tq, wide), lambda b, hh, i: (b * nq + i, COL_Q // wide + hh)),
            pl.BlockSpec((seq, wide), lambda b, hh, i: (b, COL_K // wide + hh)),
            pl.BlockSpec((seq, wide), lambda b, hh, i: (b, COL_V // wide + hh)),
            pl.BlockSpec((4, ATTN_HEAD_DIM), lambda b, hh, i: (0, 0)),
            pl.BlockSpec((1, LANES), lambda b, hh, i: (0, 0)),
        ],
        out_specs=pl.BlockSpec((tq, wide), lambda b, hh, i: (b * nq + i, hh)),
        scratch_shapes=[
            pltpu.VMEM((hp, LANES, 2 * tq), BF16),
            pltpu.VMEM((hp, nq, LANES + BF16_ROWS, tq), BF16),
            pltpu.VMEM((hp, 1, 2 * tq), F32),
            pltpu.VMEM((hp, LANES + BF16_ROWS, 2 * tq), F32),
            pltpu.VMEM((hp, tq, 2 * tq), F32),
            pltpu.VMEM((hp, 1, 2 * tq), F32),
        ],
        compiler_params=_params(("parallel", "parallel", "arbitrary")),
        name="diff_attention",
    )(proj, proj, proj, lambda_qk, subln)


def _causal_conv(x, carry, w, width):
    row = lax.broadcasted_iota(I32, carry.shape, 0)
    out = x * w[width - 1:width, :]
    for k in range(1, width):
        xr = pltpu.roll(x, k, 0)
        cr = pltpu.roll(carry, k, 0)
        head = jnp.where(row < k, cr, xr[0:SUBLANES, :])
        xk = jnp.concatenate([head, xr[SUBLANES:, :]], axis=0)
        out = out + xk * w[width - 1 - k:width - k, :]
    return out


def _ssd_kernel(z_ref, xbc_ref, gb_ref, gc_ref, u_ref, dt_ref, cw_ref, cb_ref, dtb_ref, alog_ref,
                dskip_ref, ng_ref, sw_ref, o_ref,
                cx_ref, cs_ref, state_ref, xs_ref, bm_ref, cm_ref, *, tc):
    L = SSD_CHUNK
    gw = GROUP_WIDTH
    ns = SSD_STATE

    @pl.when(pl.program_id(1) == 0)
    def _():
        cx_ref[...] = jnp.zeros(cx_ref.shape, F32)
        cs_ref[...] = jnp.zeros(cs_ref.shape, F32)
        state_ref[...] = jnp.zeros(state_ref.shape, F32)

    pu = gc_ref[...].astype(F32) * u_ref[...].astype(F32)
    sconv = gb_ref[...].astype(F32) * _causal_conv(pu, cs_ref[...], sw_ref[...], SC_CONV)
    cs_ref[...] = pu[tc - SUBLANES:, :]
    o_ref[:, gw:] = sconv.astype(BF16)

    xbc = xbc_ref[...].astype(F32)
    act = _silu(_causal_conv(xbc, cx_ref[...], cw_ref[...], SSD_CONV) + cb_ref[...])
    cx_ref[...] = xbc[tc - SUBLANES:, :]
    xs_ref[...] = act[:, 0:gw]
    bm_ref[...] = act[:, gw:gw + SSD_GROUPS * ns]
    cm_ref[...] = act[:, gw + SSD_GROUPS * ns:]

    lane = lax.broadcasted_iota(I32, (L, LANES), 1)
    lo = lane < SSD_HEAD_DIM
    rr = lax.broadcasted_iota(I32, (L, L), 0)
    cc = lax.broadcasted_iota(I32, (L, L), 1)
    tril = cc <= rr
    ltri = jnp.where(tril, 1.0, 0.0).astype(BF16)
    head_lane = lane < SSD_HEADS
    a_row = -jnp.exp(alog_ref[...])

    def pair_pattern(mat, h0):
        lo_b = lo[0:mat.shape[0], :]
        return jnp.where(lo_b, mat[:, h0:h0 + 1], mat[:, h0 + 1:h0 + 2])

    def chunk_body(c, carry):
        r0 = pl.multiple_of(c * L, L)
        rows = pl.ds(r0, L)
        dt = _softplus(dt_ref[rows, :] + dtb_ref[...])
        da = jnp.where(head_lane, dt * a_row, 0.0)
        d_hi, d_mid, d_lo = _split3(da)
        acum = _dot(ltri, d_hi) + _dot(ltri, d_mid) + _dot(ltri, d_lo)
        acum_t = acum.T
        a_last = acum[L - 1:L, :]
        ys = []
        for g in range(SSD_GROUPS):
            bg = bm_ref[rows, g * ns:(g + 1) * ns]
            cg = cm_ref[rows, g * ns:(g + 1) * ns].astype(BF16)
            cb = _dot_nt(cg, bg.astype(BF16))
            bg_t = bg.T.astype(BF16)
            for jp in range(SSD_HEADS // SSD_GROUPS // 2):
                j = g * (SSD_HEADS // SSD_GROUPS // 2) + jp
                h0 = 2 * j
                xs = xs_ref[rows, j * LANES:(j + 1) * LANES]
                xdt = xs * pair_pattern(dt, h0)
                y = None
                for hh, keep in ((h0, lo), (h0 + 1, jnp.logical_not(lo))):
                    seg = acum[:, hh:hh + 1] - acum_t[hh:hh + 1, :]
                    dec = jnp.exp(jnp.where(tril, seg, NEG_BIG))
                    mm = (dec * cb).astype(BF16)
                    part = _dot(mm, jnp.where(keep, xdt, 0.0).astype(BF16))
                    y = part if y is None else y + part
                st = state_ref[j]
                y = y + _dot(cg, st.astype(BF16)) * jnp.exp(pair_pattern(acum, h0))
                dte = jnp.exp(pair_pattern(a_last - acum, h0))
                contrib = _dot(bg_t, (xdt * dte).astype(BF16))
                state_ref[j] = st * jnp.exp(pair_pattern(a_last, h0)) + contrib
                ys.append(y + dskip_ref[:, j * LANES:(j + 1) * LANES] * xs)
        yv = jnp.concatenate(ys, axis=-1)
        z = z_ref[rows, :].astype(F32)
        vv = yv * _silu(z)
        gwid = gw // SSD_GROUPS
        outs = []
        for g in range(SSD_GROUPS):
            vg = vv[:, g * gwid:(g + 1) * gwid]
            outs.append(vg * lax.rsqrt(jnp.mean(vg * vg, axis=-1, keepdims=True) + EPS))
        o_ref[rows, 0:gw] = (jnp.concatenate(outs, axis=-1) * ng_ref[...]).astype(BF16)
        return carry

    lax.fori_loop(0, tc // L, chunk_body, 0)


def _ssd_sconv(proj, dt_raw, cw, cb, dtb, alog, dskip, ng, sw, batch, seq):
    n = proj.shape[0]
    tc = _pick(seq, 512)
    nt = seq // tc
    row = lambda b, t: b * nt + t
    const = lambda b, t: (0, 0)
    return pl.pallas_call(
        functools.partial(_ssd_kernel, tc=tc),
        out_shape=jax.ShapeDtypeStruct((n, 2 * GROUP_WIDTH), BF16),
        grid=(batch, nt),
        in_specs=[
            pl.BlockSpec((tc, GROUP_WIDTH), lambda b, t: (row(b, t), COL_Z // GROUP_WIDTH)),
            pl.BlockSpec((tc, SSD_XBC), lambda b, t: (row(b, t), COL_XBC // SSD_XBC)),
            pl.BlockSpec((tc, GROUP_WIDTH), lambda b, t: (row(b, t), COL_GB // GROUP_WIDTH)),
            pl.BlockSpec((tc, GROUP_WIDTH), lambda b, t: (row(b, t), COL_GC // GROUP_WIDTH)),
            pl.BlockSpec((tc, GROUP_WIDTH), lambda b, t: (row(b, t), COL_U // GROUP_WIDTH)),
            pl.BlockSpec((tc, LANES), lambda b, t: (row(b, t), 0)),
            pl.BlockSpec((SSD_CONV, SSD_XBC), const),
            pl.BlockSpec((1, SSD_XBC), const),
            pl.BlockSpec((1, LANES), const),
            pl.BlockSpec((1, LANES), const),
            pl.BlockSpec((1, GROUP_WIDTH), const),
            pl.BlockSpec((1, GROUP_WIDTH), const),
            pl.BlockSpec((SC_CONV, GROUP_WIDTH), const),
        ],
        out_specs=pl.BlockSpec((tc, 2 * GROUP_WIDTH), lambda b, t: (row(b, t), 0)),
        scratch_shapes=[
            pltpu.VMEM((SUBLANES, SSD_XBC), F32),
            pltpu.VMEM((SUBLANES, GROUP_WIDTH), F32),
            pltpu.VMEM((SSD_HEADS // 2, SSD_STATE, LANES), F32),
            pltpu.VMEM((tc, GROUP_WIDTH), F32),
            pltpu.VMEM((tc, SSD_GROUPS * SSD_STATE), F32),
            pltpu.VMEM((tc, SSD_GROUPS * SSD_STATE), F32),
        ],
        compiler_params=_params(("parallel", "arbitrary")),
        name="ssd_sconv",
    )(proj, proj, proj, proj, proj, dt_raw, cw, cb, dtb, alog, dskip, ng, sw)


def _mixer_residual(x, attn, sc, wo_ref, mod_ref, gmix_ref):
    gw = GROUP_WIDTH
    y = _dot(attn, wo_ref[0:gw, :]) + _dot(sc, wo_ref[gw:, :])
    return x + mod_ref[2:3, :] * _rms(y, gmix_ref[...])


def _mixer_specs(tm, d, bpb):
    return [
        pl.BlockSpec((tm, d), lambda i: (i, 0)),
        pl.BlockSpec((tm, GROUP_WIDTH), lambda i: (i, 0)),
        pl.BlockSpec((tm, 2 * GROUP_WIDTH), lambda i: (i, 0)),
        pl.BlockSpec((3 * GROUP_WIDTH, d), lambda i: (0, 0), **RESIDENT),
        pl.BlockSpec((None, 6, d), lambda i: (i // bpb, 0, 0)),
        pl.BlockSpec((1, d), lambda i: (0, 0)),
    ]


def _ffn_kernel(x_ref, a_ref, sc_ref, wo_ref, mod_ref, gmix_ref, gpre_ref, gpost_ref,
                w1_ref, w3_ref, w2_ref, o_ref):
    rows = x_ref.shape[0] // ROW_SPLIT
    for r in range(ROW_SPLIT):
        sl = slice(r * rows, (r + 1) * rows)
        x = _mixer_residual(x_ref[sl, :], a_ref[sl, :], sc_ref[sl, :], wo_ref, mod_ref, gmix_ref)
        h = (_rms(x, gpre_ref[...]) * (1.0 + mod_ref[4:5, :]) + mod_ref[3:4, :]).astype(BF16)
        t = (_silu(_dot(h, w1_ref[...])) * _dot(h, w3_ref[...])).astype(BF16)
        y = _dot(t, w2_ref[...])
        o_ref[sl, :] = x + mod_ref[5:6, :] * _rms(y, gpost_ref[...])


def _ffn(x2, attn, sc, w_out, mod_l, gmix, gpre, gpost, w1, w3, w2, rows_per_batch):
    n, d = x2.shape
    dff = w1.shape[1]
    tm = _pick(rows_per_batch, 512)
    bpb = rows_per_batch // tm
    return pl.pallas_call(
        _ffn_kernel,
        out_shape=jax.ShapeDtypeStruct((n, d), F32),
        grid=(n // tm,),
        in_specs=_mixer_specs(tm, d, bpb) + [
            pl.BlockSpec((1, d), lambda i: (0, 0)),
            pl.BlockSpec((1, d), lambda i: (0, 0)),
            pl.BlockSpec((d, dff), lambda i: (0, 0), **RESIDENT),
            pl.BlockSpec((d, dff), lambda i: (0, 0), **RESIDENT),
            pl.BlockSpec((dff, d), lambda i: (0, 0), **RESIDENT),
        ],
        out_specs=pl.BlockSpec((tm, d), lambda i: (i, 0)),
        compiler_params=_params(("parallel",)),
        name="dense_ffn",
    )(x2, attn, sc, w_out, mod_l, gmix, gpre, gpost, w1, w3, w2)


MOE_TM = 512
DSP_TM = 256
DSP_WIN = 256
DSP_DEPTH = 4
CMB_T = 512
CMB_WIN = 256
CMB_MAX_SUB = (CMB_T + BF16_ROWS + CMB_WIN - 1) // CMB_WIN


def _router_kernel(x_ref, a_ref, sc_ref, wo_ref, mod_ref, gmix_ref, g_ref, wr_ref,
                   xo_ref, h_ref, route_ref, wcum_ref, run_ref):
    @pl.when(pl.program_id(0) == 0)
    def _():
        run_ref[...] = jnp.zeros(run_ref.shape, F32)

    x = _mixer_residual(x_ref[...], a_ref[...], sc_ref[...], wo_ref, mod_ref, gmix_ref)
    xo_ref[...] = x
    h = _rms(x, g_ref[...]) * (1.0 + mod_ref[4:5, :]) + mod_ref[3:4, :]
    hh, hl = _split2(h)
    h_ref[...] = hh
    a = _dot(hh, wr_ref[...])
    b = _dot(hl, wr_ref[...])
    ne = N_EXPERTS
    logits = a + b + pltpu.roll(a, LANES - ne, 1)
    lane = lax.broadcasted_iota(I32, logits.shape, 1)
    valid = lane < ne
    l1 = jnp.max(jnp.where(valid, logits, NEG_BIG), axis=-1, keepdims=True)
    e1 = jnp.min(jnp.where(valid & (logits == l1), lane, LANES), axis=-1, keepdims=True)
    rest = valid & (lane != e1)
    l2 = jnp.max(jnp.where(rest, logits, NEG_BIG), axis=-1, keepdims=True)
    e2 = jnp.min(jnp.where(rest & (logits == l2), lane, LANES), axis=-1, keepdims=True)
    w = jnp.exp(l2 - l1)
    g1 = 1.0 / (1.0 + w)
    g2 = w / (1.0 + w)

    tm = logits.shape[0]
    sel = jnp.where(lane == e1, 1.0, jnp.where(lane == e2, 1.0, 0.0))
    rr = lax.broadcasted_iota(I32, (tm, tm), 0)
    cc = lax.broadcasted_iota(I32, (tm, tm), 1)
    before = jnp.where(cc < rr, 1.0, 0.0).astype(BF16)
    excl = _dot(before, sel.astype(BF16)) + run_ref[...]
    rank1 = jnp.sum(jnp.where(lane == e1, excl, 0.0), axis=-1, keepdims=True)
    rank2 = jnp.sum(jnp.where(lane == e2, excl, 0.0), axis=-1, keepdims=True)
    total = excl + sel
    for wdx in range(tm // DSP_WIN):
        wcum_ref[wdx:wdx + 1, :] = total[(wdx + 1) * DSP_WIN - 1:(wdx + 1) * DSP_WIN, :]
    run_ref[...] = total[tm - 1:tm, :]

    lane8 = lax.broadcasted_iota(I32, route_ref.shape, 1)
    cols = (e1.astype(F32), e2.astype(F32), g1, g2, rank1, rank2)
    out = jnp.zeros(route_ref.shape, F32)
    for idx, col in enumerate(cols):
        out = jnp.where(lane8 == idx, col, out)
    route_ref[...] = out


def _router(x2, attn, sc, w_out, mod_l, gmix, g, wr_packed, rows_per_batch):
    n, d = x2.shape
    tm = _pick(rows_per_batch, 512)
    assert tm % DSP_WIN == 0
    bpb = rows_per_batch // tm
    return pl.pallas_call(
        _router_kernel,
        out_shape=(jax.ShapeDtypeStruct((n, d), F32), jax.ShapeDtypeStruct((n, d), BF16),
                   jax.ShapeDtypeStruct((n, SUBLANES), F32),
                   jax.ShapeDtypeStruct((n // tm, tm // DSP_WIN, LANES), F32)),
        grid=(n // tm,),
        in_specs=_mixer_specs(tm, d, bpb) + [
            pl.BlockSpec((1, d), lambda i: (0, 0)),
            pl.BlockSpec((d, LANES), lambda i: (0, 0)),
        ],
        out_specs=(pl.BlockSpec((tm, d), lambda i: (i, 0)),
                   pl.BlockSpec((tm, d), lambda i: (i, 0)),
                   pl.BlockSpec((tm, SUBLANES), lambda i: (i, 0)),
                   pl.BlockSpec((None, tm // DSP_WIN, LANES), lambda i: (i, 0, 0))),
        scratch_shapes=[pltpu.VMEM((1, LANES), F32)],
        compiler_params=_params(("arbitrary",)),
        name="moe_router",
    )(x2, attn, sc, w_out, mod_l, gmix, g, wr_packed)


def _dispatch_kernel(meta_ref, wc_ref, h_hbm, pos_ref, gt_ref, xb_ref, gate_ref,
                     buf_ref, sem_ref, acc_ref, gacc_ref, st_ref):
    b = pl.program_id(0)
    nb = pl.num_programs(0)
    ne = N_EXPERTS
    tm, win = DSP_TM, DSP_WIN
    nslot = DSP_DEPTH + 1
    nwt = pos_ref.shape[0]

    def copy(w, slot):
        return pltpu.make_async_copy(h_hbm.at[pl.ds(pl.multiple_of(w * win, win), win), :],
                                     buf_ref.at[slot], sem_ref.at[slot])

    def step_range(s):
        p = s * tm
        e = jnp.zeros((), I32)
        for j in range(ne - 1):
            e = e + (p >= meta_ref[2 * ne + j]).astype(I32)
        r0 = p - meta_ref[e]
        cnt = meta_ref[ne + e]
        r1 = jnp.minimum(r0 + tm, cnt)

        def first_window_reaching(target):
            def halve(_, c):
                lo, hi = c
                mid = (lo + hi) // 2
                less = wc_ref[e * nwt + mid] < target
                return jnp.where(less, mid + 1, lo), jnp.where(less, hi, mid)

            return lax.fori_loop(0, max(nwt - 1, 1).bit_length(), halve,
                                 (jnp.zeros((), I32), jnp.full((), nwt - 1, I32)))[0]

        w_first = first_window_reaching(r0 + 1)
        w_last = first_window_reaching(r1)
        has = r0 < cnt
        return jnp.where(has, w_first, 0), jnp.where(has, w_last - w_first + 1, 0)

    def produce():
        def exhausted(c):
            return (c[0] < nb) & (c[1] >= c[3])

        def next_step(c):
            s = c[0] + 1
            w_first, count = step_range(jnp.minimum(s, nb - 1))
            return s, jnp.zeros_like(c[1]), w_first, count

        ps, pk, pw, pn = lax.while_loop(exhausted, next_step,
                                        (st_ref[0], st_ref[1], st_ref[4], st_ref[5]))
        st_ref[0] = ps
        st_ref[4] = pw
        st_ref[5] = pn

        @pl.when(ps < nb)
        def _():
            issued = st_ref[2]
            copy(pw + pk, issued % nslot).start()
            st_ref[1] = pk + 1
            st_ref[2] = issued + 1

        @pl.when(ps >= nb)
        def _():
            st_ref[1] = pk

    w0, nw = step_range(b)

    @pl.when(b == 0)
    def _():
        for idx in range(4):
            st_ref[idx] = 0
        st_ref[4] = w0
        st_ref[5] = nw
        for _ in range(DSP_DEPTH):
            produce()

    g0 = st_ref[3]
    acc_ref[...] = jnp.zeros(acc_ref.shape, F32)
    gacc_ref[...] = jnp.zeros(gacc_ref.shape, F32)
    slot_id = b * tm + lax.broadcasted_iota(I32, (tm, win), 0)

    def body(k, carry):
        slot = (g0 + k) % nslot
        copy(w0 + k, slot).wait()
        produce()
        pos = pos_ref[w0 + k]
        gt = gt_ref[w0 + k]
        eq0 = pos[0:1, :] == slot_id
        eq1 = pos[1:2, :] == slot_id
        onehot = jnp.where(eq0, 1.0, jnp.where(eq1, 1.0, 0.0)).astype(BF16)
        acc_ref[...] += _dot(onehot, buf_ref[slot])
        gsel = jnp.where(eq0, gt[0:1, :], 0.0) + jnp.where(eq1, gt[1:2, :], 0.0)
        gacc_ref[...] += jnp.sum(gsel, axis=-1, keepdims=True)
        return carry

    lax.fori_loop(0, nw, body, 0)
    st_ref[3] = g0 + nw
    xb_ref[...] = acc_ref[...].astype(BF16)
    gate_ref[...] = gacc_ref[...]


def _dispatch(meta, wcum_flat, h, pos_w, gate_w, n_steps):
    n, d = h.shape
    tm, win = DSP_TM, DSP_WIN
    nwt = n // win
    return pl.pallas_call(
        _dispatch_kernel,
        out_shape=(jax.ShapeDtypeStruct((n_steps * tm, d), BF16),
                   jax.ShapeDtypeStruct((n_steps * tm, 1), F32)),
        grid_spec=pltpu.PrefetchScalarGridSpec(
            num_scalar_prefetch=2,
            grid=(n_steps,),
            in_specs=[
                pl.BlockSpec(memory_space=pl.ANY),
                pl.BlockSpec((nwt, 2, win), lambda b, w0, nw: (0, 0, 0)),
                pl.BlockSpec((nwt, 2, win), lambda b, w0, nw: (0, 0, 0)),
            ],
            out_specs=(pl.BlockSpec((tm, d), lambda b, w0, nw: (b, 0)),
                       pl.BlockSpec((tm, 1), lambda b, w0, nw: (b, 0))),
            scratch_shapes=[
                pltpu.VMEM((DSP_DEPTH + 1, win, d), BF16),
                pltpu.SemaphoreType.DMA((DSP_DEPTH + 1,)),
                pltpu.VMEM((tm, d), F32),
                pltpu.VMEM((tm, 1), F32),
                pltpu.SMEM((6,), I32),
            ],
        ),
        compiler_params=_params(("arbitrary",)),
        name="moe_dispatch",
    )(meta, wcum_flat, h, pos_w, gate_w)


def _expert_kernel(be_ref, used_ref, xb_ref, gate_ref, w1_ref, w3_ref, w2_ref, yb_ref, acc_ref):
    b = pl.program_id(0)
    j = pl.program_id(1)
    last = pl.num_programs(1) - 1
    live = b < used_ref[0]

    @pl.when((b == 0) & (j == 0))
    def _():
        acc_ref[...] = jnp.zeros(acc_ref.shape, F32)

    @pl.when(live)
    def _():
        rows = xb_ref.shape[0] // ROW_SPLIT
        for r in range(ROW_SPLIT):
            sl = slice(r * rows, (r + 1) * rows)
            xb = xb_ref[sl, :]
            t = (_silu(_dot(xb, w1_ref[...])) * _dot(xb, w3_ref[...])).astype(BF16)
            part = _dot(t, w2_ref[...])
            acc_ref[sl, :] = jnp.where(j == 0, 0.0, acc_ref[sl, :]) + part

        @pl.when(j == last)
        def _():
            yb_ref[...] = (acc_ref[...] * gate_ref[...]).astype(BF16)

    @pl.when(jnp.logical_not(live) & (j == last))
    def _():
        yb_ref[...] = jnp.zeros(yb_ref.shape, BF16)


def _experts(block_expert, used, xb, gate_buf, w1, w3, w2, n_blocks):
    d = xb.shape[1]
    dff = w1.shape[2]
    tm = MOE_TM
    tf = dff // 2 if (dff // 2) % LANES == 0 else dff
    nj = dff // tf

    def jj(b, j, be, used):
        return jnp.where(b < used[0], j, nj - 1)

    return pl.pallas_call(
        _expert_kernel,
        out_shape=jax.ShapeDtypeStruct((n_blocks * tm, d), BF16),
        grid_spec=pltpu.PrefetchScalarGridSpec(
            num_scalar_prefetch=2,
            grid=(n_blocks, nj),
            in_specs=[
                pl.BlockSpec((tm, d), lambda b, j, be, used: (b, 0)),
                pl.BlockSpec((tm, 1), lambda b, j, be, used: (b, 0)),
                pl.BlockSpec((None, d, tf), lambda b, j, be, used: (be[b], 0, jj(b, j, be, used))),
                pl.BlockSpec((None, d, tf), lambda b, j, be, used: (be[b], 0, jj(b, j, be, used))),
                pl.BlockSpec((None, tf, d), lambda b, j, be, used: (be[b], jj(b, j, be, used), 0)),
            ],
            out_specs=pl.BlockSpec((tm, d), lambda b, j, be, used: (b, 0)),
            scratch_shapes=[pltpu.VMEM((tm, d), F32)],
        ),
        compiler_params=_params(("arbitrary", "arbitrary")),
        name="moe_experts",
    )(block_expert, used, xb, gate_buf, w1, w3, w2)


def _combine_kernel(ws_ref, nsub_ref, yb_hbm, pos_ref, x_ref, mod_ref, g_ref, o_ref,
                    buf_ref, sem_ref, xbuf_ref, xsem_ref, y_ref):
    i = pl.program_id(0)
    n_steps = pl.num_programs(0)
    ne = N_EXPERTS
    t, cw = CMB_T, CMB_WIN

    def start_row(step, e, sub):
        return pl.multiple_of(ws_ref[step * ne + e] + sub * cw, BF16_ROWS)

    def copy(step, e, slot):
        return pltpu.make_async_copy(yb_hbm.at[pl.ds(start_row(step, e, 0), cw), :],
                                     buf_ref.at[slot, e], sem_ref.at[slot, e])

    @pl.when(i == 0)
    def _():
        for e in range(ne):
            copy(0, e, 0).start()

    slot = i % 2

    @pl.when(i + 1 < n_steps)
    def _():
        for e in range(ne):
            copy(i + 1, e, 1 - slot).start()

    pe = pos_ref[...]
    pos = pe[:, 0:TOP_K]
    top = pe[:, TOP_K:2 * TOP_K]
    col = lax.broadcasted_iota(I32, (t, cw), 1)

    def onehot(rel):
        return jnp.where(rel[:, 0:1] == col, 1.0, jnp.where(rel[:, 1:2] == col, 1.0, 0.0)).astype(BF16)

    rels = []
    for e in range(ne):
        copy(i, e, slot).wait()
        rels.append(jnp.where(top == e, pos - ws_ref[i * ne + e], -1))
    picks = jnp.concatenate([onehot(rel) for rel in rels], axis=1)
    y_ref[...] = _dot(picks, buf_ref[slot].reshape(ne * cw, buf_ref.shape[-1]))

    for e in range(ne):
        nsub = nsub_ref[i * ne + e]
        for sub in range(1, CMB_MAX_SUB):
            @pl.when(nsub > sub)
            def _():
                extra = pltpu.make_async_copy(yb_hbm.at[pl.ds(start_row(i, e, sub), cw), :],
                                              xbuf_ref, xsem_ref)
                extra.start()
                extra.wait()
                y_ref[...] += _dot(onehot(rels[e] - sub * cw), xbuf_ref[...])

    o_ref[...] = x_ref[...] + mod_ref[5:6, :] * _rms(y_ref[...], g_ref[...])


def _combine(win_start, nsub, yb, pos, x2, mod_l, g, rows_per_batch):
    n, d = x2.shape
    t = CMB_T
    bpb = rows_per_batch // t
    return pl.pallas_call(
        _combine_kernel,
        out_shape=jax.ShapeDtypeStruct((n, d), F32),
        grid_spec=pltpu.PrefetchScalarGridSpec(
            num_scalar_prefetch=2,
            grid=(n // t,),
            in_specs=[
                pl.BlockSpec(memory_space=pl.ANY),
                pl.BlockSpec((t, 2 * TOP_K), lambda i, ws, ns: (i, 0)),
                pl.BlockSpec((t, d), lambda i, ws, ns: (i, 0)),
                pl.BlockSpec((None, 6, d), lambda i, ws, ns: (i // bpb, 0, 0)),
                pl.BlockSpec((1, d), lambda i, ws, ns: (0, 0)),
            ],
            out_specs=pl.BlockSpec((t, d), lambda i, ws, ns: (i, 0)),
            scratch_shapes=[
                pltpu.VMEM((2, N_EXPERTS, CMB_WIN, d), BF16),
                pltpu.SemaphoreType.DMA((2, N_EXPERTS)),
                pltpu.VMEM((CMB_WIN, d), BF16),
                pltpu.SemaphoreType.DMA(()),
                pltpu.VMEM((t, d), F32),
            ],
        ),
        compiler_params=_params(("arbitrary",)),
        name="moe_combine",
    )(win_start, nsub, yb, pos, x2, mod_l, g)


def _moe(x2, attn, sc, w_out, mod_l, gmix, gpre, gpost, w_router, w1, w3, w2, rows_per_batch):
    n, d = x2.shape
    ne, tm, dtm, win, t = N_EXPERTS, MOE_TM, DSP_TM, DSP_WIN, CMB_T
    wr_hi, wr_lo = _split2(w_router)
    wr_packed = jnp.zeros((d, LANES), BF16).at[:, 0:ne].set(wr_hi).at[:, ne:2 * ne].set(wr_lo)
    x2, h, route, wcum = _router(x2, attn, sc, w_out, mod_l, gmix, gpre, wr_packed, rows_per_batch)

    top = route[:, 0:TOP_K].astype(I32)
    gates = route[:, TOP_K:2 * TOP_K]
    rank = route[:, 2 * TOP_K:3 * TOP_K].astype(I32)
    wc = wcum.reshape(n // win, LANES)[:, 0:ne].astype(I32)
    counts = wc[-1]
    padded = (counts + tm - 1) // tm * tm
    pend = jnp.cumsum(padded)
    pstart = pend - padded
    experts = jnp.arange(ne, dtype=I32)
    pos = rank + jnp.sum(jnp.where(top[:, :, None] == experts, pstart, 0), axis=-1)

    n_blocks = (n * TOP_K) // tm + ne + 1
    blk = jnp.arange(n_blocks, dtype=I32) * tm
    block_expert = jnp.minimum(jnp.sum((pend[None, :] <= blk[:, None]).astype(I32), axis=1), ne - 1)
    used = (pend[-1] // tm).astype(I32).reshape(1)

    n_steps = n_blocks * (tm // dtm)
    meta = jnp.concatenate([pstart, counts, pend]).astype(I32)
    pos_w = pos.T.reshape(TOP_K, n // win, win).transpose(1, 0, 2)
    gate_w = gates.T.reshape(TOP_K, n // win, win).transpose(1, 0, 2)
    xb, gate_buf = _dispatch(meta, wc.T.reshape(-1), h, pos_w, gate_w, n_steps)
    yb = _experts(block_expert, used, xb, gate_buf, w1, w3, w2, n_blocks)

    at_block_end = wc[t // win - 1::t // win]
    before_block = jnp.concatenate([jnp.zeros((1, ne), I32), at_block_end[:-1]], axis=0)
    base_b = pstart[None, :] + before_block
    end_b = pstart[None, :] + at_block_end
    win_start = base_b // BF16_ROWS * BF16_ROWS
    nsub = jnp.where(end_b > base_b, (end_b - win_start + CMB_WIN - 1) // CMB_WIN, 0)
    return _combine(win_start.astype(I32).reshape(-1), nsub.astype(I32).reshape(-1), yb,
                    jnp.concatenate([pos, top], axis=1), x2, mod_l, gpost, rows_per_batch)


def _reorder_w_in(w):
    main = jnp.concatenate([w[:, 0:3072], w[:, 3080:4616]], axis=1).astype(BF16)
    wdt = jnp.zeros((w.shape[0], LANES), BF16).at[:, 0:SSD_HEADS].set(w[:, 3072:3080].astype(BF16))
    return main, wdt


def _pad_row(v, width=LANES):
    return jnp.zeros((1, width), F32).at[0, 0:v.shape[0]].set(v)


def kernel(x, c, w_mod, b_mod, norm_mix_pre, norm_mix_post, norm_ffn_pre, norm_ffn_post, w_in, w_out, lambda_qk, attn_subln, ssd_conv_w, ssd_conv_b, ssd_dt_bias, ssd_a_log, ssd_d, ssd_norm, sconv_w, ffn_w1, ffn_w3, ffn_w2, moe_router, moe_w1, moe_w3, moe_w2):
    batch, seq, d = x.shape
    depth = w_mod.shape[0]
    n = batch * seq
    assert seq % SSD_CHUNK == 0 and w_in.shape[2] == PROJ_COLS + SSD_HEADS
    mod = _modulation(c, w_mod, b_mod).reshape(depth, batch, 6, d)
    x2 = x.reshape(n, d)
    for i in range(depth):
        mod_l = mod[i]
        w_main, w_dt = _reorder_w_in(w_in[i])
        proj, dt_raw = _inproj(x2, mod_l, norm_mix_pre[i][None, :], w_main, w_dt, seq)
        lam_init = 0.8 - 0.6 * math.exp(-0.3 * i)
        attn = _attention(proj, lambda_qk[i], attn_subln[i][None, :], batch, seq, lam_init)
        sc = _ssd_sconv(proj, dt_raw, ssd_conv_w[i], ssd_conv_b[i][None, :], _pad_row(ssd_dt_bias[i]),
                        _pad_row(ssd_a_log[i]), jnp.repeat(ssd_d[i], SSD_HEAD_DIM)[None, :],
                        ssd_norm[i][None, :], sconv_w[i], batch, seq)
        mix = (x2, attn, sc, w_out[i].astype(BF16), mod_l, norm_mix_post[i][None, :])
        if i % 2 == 0:
            x2 = _ffn(*mix, norm_ffn_pre[i][None, :], norm_ffn_post[i][None, :],
                      ffn_w1[i // 2].astype(BF16), ffn_w3[i // 2].astype(BF16), ffn_w2[i // 2].astype(BF16), seq)
        else:
            x2 = _moe(*mix, norm_ffn_pre[i][None, :], norm_ffn_post[i][None, :], moe_router[i // 2],
                      moe_w1[i // 2].astype(BF16), moe_w3[i // 2].astype(BF16), moe_w2[i // 2].astype(BF16), seq)
    return x2.reshape(batch, seq, d)
```

```python
import functools
import math

import jax
import jax.numpy as jnp
from jax import lax
from jax.experimental import pallas as pl
from jax.experimental.pallas import tpu as pltpu

F32 = jnp.float32
BF16 = jnp.bfloat16
I32 = jnp.int32

EPS = 1e-6
GROUP_WIDTH = 512
ATTN_HEADS = 4
ATTN_HEAD_DIM = 64
SSD_HEAD_DIM = 64
SSD_HEADS = 8
SSD_GROUPS = 2
SSD_STATE = 128
SSD_CONV = 4
SSD_CHUNK = 128
SSD_XBC = GROUP_WIDTH + 2 * SSD_GROUPS * SSD_STATE
SC_CONV = 3
N_EXPERTS = 8
TOP_K = 2

LANES = 128
SUBLANES = 8
BF16_ROWS = 16
VMEM_LIMIT = 48 * 1024 * 1024
NEG_BIG = -1e30
ROW_SPLIT = 2
RESIDENT = dict(pipeline_mode=pl.Buffered(1))

COL_Q, COL_K, COL_V, COL_Z, COL_XBC, COL_GB, COL_GC, COL_U = 0, 512, 1024, 1536, 2048, 3072, 3584, 4096
PROJ_COLS = 4608

NT_DIMS = (((1,), (1,)), ((), ()))


def _dot(a, b):
    return jnp.dot(a, b, preferred_element_type=F32)


def _dot_nt(a, b):
    return lax.dot_general(a, b, NT_DIMS, preferred_element_type=F32)


def _split2(x):
    hi = x.astype(BF16)
    lo = (x - hi.astype(F32)).astype(BF16)
    return hi, lo


def _split3(x):
    hi = x.astype(BF16)
    r = x - hi.astype(F32)
    mid = r.astype(BF16)
    lo = (r - mid.astype(F32)).astype(BF16)
    return hi, mid, lo


def _sigmoid(x):
    return 0.5 * jnp.tanh(0.5 * x) + 0.5


def _silu(x):
    return x * _sigmoid(x)


def _softplus(x):
    return jnp.maximum(x, 0.0) + jnp.log(1.0 + jnp.exp(-jnp.abs(x)))


def _rms(x, g):
    return x * lax.rsqrt(jnp.mean(x * x, axis=-1, keepdims=True) + EPS) * g


def _params(sem):
    return pltpu.CompilerParams(dimension_semantics=sem, vmem_limit_bytes=VMEM_LIMIT)


def _pick(n, pref):
    t = min(n, pref)
    assert n % t == 0, (n, pref)
    return t


def _mod_kernel(c_ref, w_ref, b_ref, o_ref):
    c = c_ref[...]
    ah, al = _split2(_silu(c))
    wh, wl = _split2(w_ref[...])
    o_ref[...] = _dot(ah, wh) + _dot(ah, wl) + _dot(al, wh) + b_ref[...]


def _modulation(c, w_mod, b_mod):
    depth, d, n6 = w_mod.shape
    b = c.shape[0]
    tn = _pick(n6, 1536)
    return pl.pallas_call(
        _mod_kernel,
        out_shape=jax.ShapeDtypeStruct((depth, b, n6), F32),
        grid=(depth, n6 // tn),
        in_specs=[
            pl.BlockSpec((b, d), lambda l, j: (0, 0)),
            pl.BlockSpec((None, d, tn), lambda l, j: (l, 0, j)),
            pl.BlockSpec((None, 1, tn), lambda l, j: (l, 0, j)),
        ],
        out_specs=pl.BlockSpec((None, b, tn), lambda l, j: (l, 0, j)),
        compiler_params=_params(("parallel", "parallel")),
        name="modulation",
    )(c, w_mod, b_mod.reshape(depth, 1, n6))


def _inproj_kernel(x_ref, mod_ref, g_ref, w_ref, wdt_ref, o_ref, dt_ref):
    rows = x_ref.shape[0] // ROW_SPLIT
    for r in range(ROW_SPLIT):
        sl = slice(r * rows, (r + 1) * rows)
        h = _rms(x_ref[sl, :], g_ref[...]) * (1.0 + mod_ref[1:2, :]) + mod_ref[0:1, :]
        hb = h.astype(BF16)
        dt_ref[sl, :] = _dot(hb, wdt_ref[...])
        o_ref[sl, :] = _dot(hb, w_ref[...]).astype(BF16)


def _inproj(x2, mod_l, g, w, wdt, rows_per_batch):
    n, d = x2.shape
    tm = _pick(rows_per_batch, 512)
    bpb = rows_per_batch // tm
    return pl.pallas_call(
        _inproj_kernel,
        out_shape=(jax.ShapeDtypeStruct((n, PROJ_COLS), BF16),
                   jax.ShapeDtypeStruct((n, LANES), F32)),
        grid=(n // tm,),
        in_specs=[
            pl.BlockSpec((tm, d), lambda i: (i, 0)),
            pl.BlockSpec((None, 6, d), lambda i: (i // bpb, 0, 0)),
            pl.BlockSpec((1, d), lambda i: (0, 0)),
            pl.BlockSpec((d, PROJ_COLS), lambda i: (0, 0), **RESIDENT),
            pl.BlockSpec((d, LANES), lambda i: (0, 0), **RESIDENT),
        ],
        out_specs=(pl.BlockSpec((tm, PROJ_COLS), lambda i: (i, 0)),
                   pl.BlockSpec((tm, LANES), lambda i: (i, 0))),
        compiler_params=_params(("parallel",)),
        name="inproj",
    )(x2, mod_l, g, w, wdt)


ATTN_PAIR = 2


def _attn_kernel(q_ref, k_ref, v_ref, lq_ref, sub_ref, o_ref, qt_ref, vt_ref, m_ref, acc_ref,
                 s_ref, cmax_ref, *, tq, lam_init):
    i = pl.program_id(2)
    d = ATTN_HEAD_DIM
    n_chunks = vt_ref.shape[1]
    heads = range(ATTN_PAIR)

    def lanes(hd):
        return slice(hd * LANES, (hd + 1) * LANES)

    @pl.when(i == 0)
    def _():
        ones_row = jnp.where(lax.broadcasted_iota(I32, (BF16_ROWS, tq), 0) == 0, 1.0, 0.0).astype(BF16)
        for hd in heads:
            for cidx in range(n_chunks):
                vt = v_ref[cidx * tq:(cidx + 1) * tq, lanes(hd)].astype(F32).T.astype(BF16)
                vt_ref[hd, cidx, 0:LANES, :] = vt
                vt_ref[hd, cidx, LANES:, :] = ones_row

    for hd in heads:
        qt = (q_ref[:, lanes(hd)].astype(F32) * (d ** -0.5 * math.log2(math.e))).T
        row = lax.broadcasted_iota(I32, qt.shape, 0)
        qt_ref[hd, :, 0:tq] = jnp.where(row < d, qt, 0.0).astype(BF16)
        qt_ref[hd, :, tq:] = jnp.where(row >= d, qt, 0.0).astype(BF16)
    m_ref[...] = jnp.full(m_ref.shape, NEG_BIG, F32)
    acc_ref[...] = jnp.zeros(acc_ref.shape, F32)

    def scores(hd, j, masked):
        start = pl.multiple_of(j * tq, tq)
        s = _dot(k_ref[pl.ds(start, tq), lanes(hd)], qt_ref[hd])
        if masked:
            kk = lax.broadcasted_iota(I32, (tq, tq), 0)
            qq = lax.broadcasted_iota(I32, (tq, tq), 1)
            keep = kk <= qq
            s = jnp.where(jnp.concatenate([keep, keep], axis=1), s, NEG_BIG)
        s_ref[hd] = s
        cmax_ref[hd] = jnp.max(s, axis=0, keepdims=True)

    def accumulate(hd, j):
        m_prev = m_ref[hd]
        m_new = jnp.maximum(m_prev, cmax_ref[hd])
        alpha = jnp.exp2(m_prev - m_new)
        p = jnp.exp2(s_ref[hd] - m_new)
        acc_ref[hd] = alpha * acc_ref[hd] + _dot(vt_ref[hd, j], p.astype(BF16))
        m_ref[hd] = m_new

    scores(0, i, True)
    scores(1, i, True)
    accumulate(0, i)

    def chunk_step(j):
        scores(0, j, False)
        accumulate(1, jnp.where(j == 0, i, j - 1))
        scores(1, j, False)
        accumulate(0, j)

    def body(u, carry):
        chunk_step(2 * u)
        chunk_step(2 * u + 1)
        return carry

    lax.fori_loop(0, i // 2, body, 0)

    @pl.when(i % 2 == 1)
    def _():
        chunk_step(i - 1)

    accumulate(1, jnp.where(i == 0, i, i - 1))

    lq = lq_ref[...]
    lam = (jnp.exp(jnp.sum(lq[0:1, :] * lq[1:2, :], axis=-1, keepdims=True))
           - jnp.exp(jnp.sum(lq[2:3, :] * lq[3:4, :], axis=-1, keepdims=True)) + lam_init)
    for hd in heads:
        inv = 1.0 / acc_ref[hd, LANES:LANES + 1, :]
        ot = (acc_ref[hd, 0:LANES, 0:tq] * inv[:, 0:tq]
              - lam * (acc_ref[hd, 0:LANES, tq:] * inv[:, tq:]))
        ot = ot * lax.rsqrt(jnp.mean(ot * ot, axis=0, keepdims=True) + EPS)
        o_ref[:, lanes(hd)] = (ot.T * sub_ref[...] * (1.0 - lam_init)).astype(BF16)


def _attention(proj, lambda_qk, subln, batch, seq, lam_init):
    n = proj.shape[0]
    tq = _pick(seq, 512)
    nq = seq // tq
    hp = ATTN_PAIR
    wide = hp * LANES
    assert ATTN_PAIR == 2 and ATTN_HEADS % hp == 0
    return pl.pallas_call(
        functools.partial(_attn_kernel, tq=tq, lam_init=lam_init),
        out_shape=jax.ShapeDtypeStruct((n, GROUP_WIDTH), BF16),
        grid=(batch, ATTN_HEADS // hp, nq),
        in_specs=[
            pl.BlockSpec((tq, wide), lambda b, hh, i: (b * nq + i, COL_Q // wide + hh)),
            pl.BlockSpec((seq, wide), lambda b, hh, i: (b, COL_K // wide + hh)),
            pl.BlockSpec((seq, wide), lambda b, hh, i: (b, COL_V // wide + hh)),
            pl.BlockSpec((4, ATTN_HEAD_DIM), lambda b, hh, i: (0, 0)),
            pl.BlockSpec((1, LANES), lambda b, hh, i: (0, 0)),
        ],
        out_specs=pl.BlockSpec((tq, wide), lambda b, hh, i: (b * nq + i, hh)),
        scratch_shapes=[
            pltpu.VMEM((hp, LANES, 2 * tq), BF16),
            pltpu.VMEM((hp, nq, LANES + BF16_ROWS, tq), BF16),
            pltpu.VMEM((hp, 1, 2 * tq), F32),
            pltpu.VMEM((hp, LANES + BF16_ROWS, 2 * tq), F32),
            pltpu.VMEM((hp, tq, 2 * tq), F32),
            pltpu.VMEM((hp, 1, 2 * tq), F32),
        ],
        compiler_params=_params(("parallel", "parallel", "arbitrary")),
        name="diff_attention",
    )(proj, proj, proj, lambda_qk, subln)


def _causal_conv(x, carry, w, width):
    row = lax.broadcasted_iota(I32, carry.shape, 0)
    out = x * w[width - 1:width, :]
    for k in range(1, width):
        xr = pltpu.roll(x, k, 0)
        cr = pltpu.roll(carry, k, 0)
        head = jnp.where(row < k, cr, xr[0:SUBLANES, :])
        xk = jnp.concatenate([head, xr[SUBLANES:, :]], axis=0)
        out = out + xk * w[width - 1 - k:width - k, :]
    return out


def _ssd_kernel(z_ref, xbc_ref, gb_ref, gc_ref, u_ref, dt_ref, cw_ref, cb_ref, dtb_ref, alog_ref,
                dskip_ref, ng_ref, sw_ref, o_ref,
                cx_ref, cs_ref, state_ref, xs_ref, bm_ref, cm_ref, *, tc):
    L = SSD_CHUNK
    gw = GROUP_WIDTH
    ns = SSD_STATE

    @pl.when(pl.program_id(1) == 0)
    def _():
        cx_ref[...] = jnp.zeros(cx_ref.shape, F32)
        cs_ref[...] = jnp.zeros(cs_ref.shape, F32)
        state_ref[...] = jnp.zeros(state_ref.shape, F32)

    pu = gc_ref[...].astype(F32) * u_ref[...].astype(F32)
    sconv = gb_ref[...].astype(F32) * _causal_conv(pu, cs_ref[...], sw_ref[...], SC_CONV)
    cs_ref[...] = pu[tc - SUBLANES:, :]
    o_ref[:, gw:] = sconv.astype(BF16)

    xbc = xbc_ref[...].astype(F32)
    act = _silu(_causal_conv(xbc, cx_ref[...], cw_ref[...], SSD_CONV) + cb_ref[...])
    cx_ref[...] = xbc[tc - SUBLANES:, :]
    xs_ref[...] = act[:, 0:gw]
    bm_ref[...] = act[:, gw:gw + SSD_GROUPS * ns]
    cm_ref[...] = act[:, gw + SSD_GROUPS * ns:]

    lane = lax.broadcasted_iota(I32, (L, LANES), 1)
    lo = lane < SSD_HEAD_DIM
    rr = lax.broadcasted_iota(I32, (L, L), 0)
    cc = lax.broadcasted_iota(I32, (L, L), 1)
    tril = cc <= rr
    ltri = jnp.where(tril, 1.0, 0.0).astype(BF16)
    head_lane = lane < SSD_HEADS
    a_row = -jnp.exp(alog_ref[...])

    def pair_pattern(mat, h0):
        lo_b = lo[0:mat.shape[0], :]
        return jnp.where(lo_b, mat[:, h0:h0 + 1], mat[:, h0 + 1:h0 + 2])

    def chunk_body(c, carry):
        rows = slice(c * L, (c + 1) * L)
        dt = _softplus(dt_ref[rows, :] + dtb_ref[...])
        da = jnp.where(head_lane, dt * a_row, 0.0)
        d_hi, d_mid, d_lo = _split3(da)
        acum = _dot(ltri, d_hi) + _dot(ltri, d_mid) + _dot(ltri, d_lo)
        acum_t = acum.T
        a_last = acum[L - 1:L, :]
        ys = []
        for g in range(SSD_GROUPS):
            bg = bm_ref[rows, g * ns:(g + 1) * ns]
            cg = cm_ref[rows, g * ns:(g + 1) * ns].astype(BF16)
            cb = _dot_nt(cg, bg.astype(BF16))
            bg_t = bg.T.astype(BF16)
            for jp in range(SSD_HEADS // SSD_GROUPS // 2):
                j = g * (SSD_HEADS // SSD_GROUPS // 2) + jp
                h0 = 2 * j
                xs = xs_ref[rows, j * LANES:(j + 1) * LANES]
                xdt = xs * pair_pattern(dt, h0)
                cols = [jnp.broadcast_to(acum[:, hh:hh + 1], (L, LANES)) for hh in (h0, h0 + 1)]
                acum_pat = jnp.where(lo, cols[0], cols[1])
                alast_pat = pair_pattern(a_last, h0)
                y = None
                for hh, col, keep in ((h0, cols[0], lo), (h0 + 1, cols[1], jnp.logical_not(lo))):
                    seg = col - acum_t[hh:hh + 1, :]
                    dec = jnp.exp(jnp.where(tril, seg, NEG_BIG))
                    mm = (dec * cb).astype(BF16)
                    part = _dot(mm, jnp.where(keep, xdt, 0.0).astype(BF16))
                    y = part if y is None else y + part
                st = state_ref[j]
                y = y + _dot(cg, st.astype(BF16)) * jnp.exp(acum_pat)
                dte = jnp.exp(alast_pat - acum_pat)
                contrib = _dot(bg_t, (xdt * dte).astype(BF16))
                state_ref[j] = st * jnp.exp(alast_pat) + contrib
                ys.append(y + dskip_ref[:, j * LANES:(j + 1) * LANES] * xs)
        yv = jnp.concatenate(ys, axis=-1)
        z = z_ref[rows, :].astype(F32)
        vv = yv * _silu(z)
        gwid = gw // SSD_GROUPS
        outs = []
        for g in range(SSD_GROUPS):
            vg = vv[:, g * gwid:(g + 1) * gwid]
            outs.append(vg * lax.rsqrt(jnp.mean(vg * vg, axis=-1, keepdims=True) + EPS))
        o_ref[rows, 0:gw] = (jnp.concatenate(outs, axis=-1) * ng_ref[...]).astype(BF16)
        return carry

    for c in range(tc // L):
        chunk_body(c, 0)


def _ssd_sconv(proj, dt_raw, cw, cb, dtb, alog, dskip, ng, sw, batch, seq):
    n = proj.shape[0]
    tc = _pick(seq, 512)
    nt = seq // tc
    row = lambda b, t: b * nt + t
    const = lambda b, t: (0, 0)
    return pl.pallas_call(
        functools.partial(_ssd_kernel, tc=tc),
        out_shape=jax.ShapeDtypeStruct((n, 2 * GROUP_WIDTH), BF16),
        grid=(batch, nt),
        in_specs=[
            pl.BlockSpec((tc, GROUP_WIDTH), lambda b, t: (row(b, t), COL_Z // GROUP_WIDTH)),
            pl.BlockSpec((tc, SSD_XBC), lambda b, t: (row(b, t), COL_XBC // SSD_XBC)),
            pl.BlockSpec((tc, GROUP_WIDTH), lambda b, t: (row(b, t), COL_GB // GROUP_WIDTH)),
            pl.BlockSpec((tc, GROUP_WIDTH), lambda b, t: (row(b, t), COL_GC // GROUP_WIDTH)),
            pl.BlockSpec((tc, GROUP_WIDTH), lambda b, t: (row(b, t), COL_U // GROUP_WIDTH)),
            pl.BlockSpec((tc, LANES), lambda b, t: (row(b, t), 0)),
            pl.BlockSpec((SSD_CONV, SSD_XBC), const),
            pl.BlockSpec((1, SSD_XBC), const),
            pl.BlockSpec((1, LANES), const),
            pl.BlockSpec((1, LANES), const),
            pl.BlockSpec((1, GROUP_WIDTH), const),
            pl.BlockSpec((1, GROUP_WIDTH), const),
            pl.BlockSpec((SC_CONV, GROUP_WIDTH), const),
        ],
        out_specs=pl.BlockSpec((tc, 2 * GROUP_WIDTH), lambda b, t: (row(b, t), 0)),
        scratch_shapes=[
            pltpu.VMEM((SUBLANES, SSD_XBC), F32),
            pltpu.VMEM((SUBLANES, GROUP_WIDTH), F32),
            pltpu.VMEM((SSD_HEADS // 2, SSD_STATE, LANES), F32),
            pltpu.VMEM((tc, GROUP_WIDTH), F32),
            pltpu.VMEM((tc, SSD_GROUPS * SSD_STATE), F32),
            pltpu.VMEM((tc, SSD_GROUPS * SSD_STATE), F32),
        ],
        compiler_params=_params(("parallel", "arbitrary")),
        name="ssd_sconv",
    )(proj, proj, proj, proj, proj, dt_raw, cw, cb, dtb, alog, dskip, ng, sw)


def _mixer_residual(x, attn, sc, wo_ref, mod_ref, gmix_ref):
    gw = GROUP_WIDTH
    y = _dot(attn, wo_ref[0:gw, :]) + _dot(sc, wo_ref[gw:, :])
    return x + mod_ref[2:3, :] * _rms(y, gmix_ref[...])


def _mixer_specs(tm, d, bpb):
    return [
        pl.BlockSpec((tm, d), lambda i: (i, 0)),
        pl.BlockSpec((tm, GROUP_WIDTH), lambda i: (i, 0)),
        pl.BlockSpec((tm, 2 * GROUP_WIDTH), lambda i: (i, 0)),
        pl.BlockSpec((3 * GROUP_WIDTH, d), lambda i: (0, 0), **RESIDENT),
        pl.BlockSpec((None, 6, d), lambda i: (i // bpb, 0, 0)),
        pl.BlockSpec((1, d), lambda i: (0, 0)),
    ]


def _ffn_kernel(x_ref, a_ref, sc_ref, wo_ref, mod_ref, gmix_ref, gpre_ref, gpost_ref,
                w1_ref, w3_ref, w2_ref, o_ref):
    rows = x_ref.shape[0] // ROW_SPLIT
    for r in range(ROW_SPLIT):
        sl = slice(r * rows, (r + 1) * rows)
        x = _mixer_residual(x_ref[sl, :], a_ref[sl, :], sc_ref[sl, :], wo_ref, mod_ref, gmix_ref)
        h = (_rms(x, gpre_ref[...]) * (1.0 + mod_ref[4:5, :]) + mod_ref[3:4, :]).astype(BF16)
        t = (_silu(_dot(h, w1_ref[...])) * _dot(h, w3_ref[...])).astype(BF16)
        y = _dot(t, w2_ref[...])
        o_ref[sl, :] = x + mod_ref[5:6, :] * _rms(y, gpost_ref[...])


def _ffn(x2, attn, sc, w_out, mod_l, gmix, gpre, gpost, w1, w3, w2, rows_per_batch):
    n, d = x2.shape
    dff = w1.shape[1]
    tm = _pick(rows_per_batch, 512)
    bpb = rows_per_batch // tm
    return pl.pallas_call(
        _ffn_kernel,
        out_shape=jax.ShapeDtypeStruct((n, d), F32),
        grid=(n // tm,),
        in_specs=_mixer_specs(tm, d, bpb) + [
            pl.BlockSpec((1, d), lambda i: (0, 0)),
            pl.BlockSpec((1, d), lambda i: (0, 0)),
            pl.BlockSpec((d, dff), lambda i: (0, 0), **RESIDENT),
            pl.BlockSpec((d, dff), lambda i: (0, 0), **RESIDENT),
            pl.BlockSpec((dff, d), lambda i: (0, 0), **RESIDENT),
        ],
        out_specs=pl.BlockSpec((tm, d), lambda i: (i, 0)),
        compiler_params=_params(("parallel",)),
        name="dense_ffn",
    )(x2, attn, sc, w_out, mod_l, gmix, gpre, gpost, w1, w3, w2)


MOE_TM = 512
DSP_TM = 256
DSP_WIN = 256
DSP_DEPTH = 6
CMB_T = 512
CMB_WIN = 256
CMB_MAX_SUB = (CMB_T + BF16_ROWS + CMB_WIN - 1) // CMB_WIN


def _router_kernel(x_ref, a_ref, sc_ref, wo_ref, mod_ref, gmix_ref, g_ref, wr_ref,
                   xo_ref, h_ref, route_ref, wcum_ref, run_ref):
    @pl.when(pl.program_id(0) == 0)
    def _():
        run_ref[...] = jnp.zeros(run_ref.shape, F32)

    x = _mixer_residual(x_ref[...], a_ref[...], sc_ref[...], wo_ref, mod_ref, gmix_ref)
    xo_ref[...] = x
    h = _rms(x, g_ref[...]) * (1.0 + mod_ref[4:5, :]) + mod_ref[3:4, :]
    hh, hl = _split2(h)
    h_ref[...] = hh
    a = _dot(hh, wr_ref[...])
    b = _dot(hl, wr_ref[...])
    ne = N_EXPERTS
    logits = a + b + pltpu.roll(a, LANES - ne, 1)
    lane = lax.broadcasted_iota(I32, logits.shape, 1)
    valid = lane < ne
    l1 = jnp.max(jnp.where(valid, logits, NEG_BIG), axis=-1, keepdims=True)
    e1 = jnp.min(jnp.where(valid & (logits == l1), lane, LANES), axis=-1, keepdims=True)
    rest = valid & (lane != e1)
    l2 = jnp.max(jnp.where(rest, logits, NEG_BIG), axis=-1, keepdims=True)
    e2 = jnp.min(jnp.where(rest & (logits == l2), lane, LANES), axis=-1, keepdims=True)
    w = jnp.exp(l2 - l1)
    g1 = 1.0 / (1.0 + w)
    g2 = w / (1.0 + w)

    tm = logits.shape[0]
    sel = jnp.where(lane == e1, 1.0, jnp.where(lane == e2, 1.0, 0.0))
    rr = lax.broadcasted_iota(I32, (tm, tm), 0)
    cc = lax.broadcasted_iota(I32, (tm, tm), 1)
    before = jnp.where(cc < rr, 1.0, 0.0).astype(BF16)
    excl = _dot(before, sel.astype(BF16)) + run_ref[...]
    rank1 = jnp.sum(jnp.where(lane == e1, excl, 0.0), axis=-1, keepdims=True)
    rank2 = jnp.sum(jnp.where(lane == e2, excl, 0.0), axis=-1, keepdims=True)
    total = excl + sel
    for wdx in range(tm // DSP_WIN):
        wcum_ref[wdx:wdx + 1, :] = total[(wdx + 1) * DSP_WIN - 1:(wdx + 1) * DSP_WIN, :]
    run_ref[...] = total[tm - 1:tm, :]

    lane8 = lax.broadcasted_iota(I32, route_ref.shape, 1)
    cols = (e1.astype(F32), e2.astype(F32), g1, g2, rank1, rank2)
    out = jnp.zeros(route_ref.shape, F32)
    for idx, col in enumerate(cols):
        out = jnp.where(lane8 == idx, col, out)
    route_ref[...] = out


def _router(x2, attn, sc, w_out, mod_l, gmix, g, wr_packed, rows_per_batch):
    n, d = x2.shape
    tm = _pick(rows_per_batch, 512)
    assert tm % DSP_WIN == 0
    bpb = rows_per_batch // tm
    return pl.pallas_call(
        _router_kernel,
        out_shape=(jax.ShapeDtypeStruct((n, d), F32), jax.ShapeDtypeStruct((n, d), BF16),
                   jax.ShapeDtypeStruct((n, SUBLANES), F32),
                   jax.ShapeDtypeStruct((n // tm, tm // DSP_WIN, LANES), F32)),
        grid=(n // tm,),
        in_specs=_mixer_specs(tm, d, bpb) + [
            pl.BlockSpec((1, d), lambda i: (0, 0)),
            pl.BlockSpec((d, LANES), lambda i: (0, 0)),
        ],
        out_specs=(pl.BlockSpec((tm, d), lambda i: (i, 0)),
                   pl.BlockSpec((tm, d), lambda i: (i, 0)),
                   pl.BlockSpec((tm, SUBLANES), lambda i: (i, 0)),
                   pl.BlockSpec((None, tm // DSP_WIN, LANES), lambda i: (i, 0, 0))),
        scratch_shapes=[pltpu.VMEM((1, LANES), F32)],
        compiler_params=_params(("arbitrary",)),
        name="moe_router",
    )(x2, attn, sc, w_out, mod_l, gmix, g, wr_packed)


def _dispatch_kernel(meta_ref, wc_ref, h_hbm, pos_ref, gt_ref, xb_ref, gate_ref,
                     buf_ref, sem_ref, acc_ref, gacc_ref, st_ref):
    b = pl.program_id(0)
    nb = pl.num_programs(0)
    ne = N_EXPERTS
    tm, win = DSP_TM, DSP_WIN
    nslot = DSP_DEPTH + 1
    nwt = pos_ref.shape[0]

    def copy(w, slot):
        return pltpu.make_async_copy(h_hbm.at[pl.ds(pl.multiple_of(w * win, win), win), :],
                                     buf_ref.at[slot], sem_ref.at[slot])

    def step_range(s):
        p = s * tm
        e = jnp.zeros((), I32)
        for j in range(ne - 1):
            e = e + (p >= meta_ref[2 * ne + j]).astype(I32)
        r0 = p - meta_ref[e]
        cnt = meta_ref[ne + e]
        r1 = jnp.minimum(r0 + tm, cnt)

        def first_window_reaching(target):
            def halve(_, c):
                lo, hi = c
                mid = (lo + hi) // 2
                less = wc_ref[e * nwt + mid] < target
                return jnp.where(less, mid + 1, lo), jnp.where(less, hi, mid)

            return lax.fori_loop(0, max(nwt - 1, 1).bit_length(), halve,
                                 (jnp.zeros((), I32), jnp.full((), nwt - 1, I32)))[0]

        w_first = first_window_reaching(r0 + 1)
        w_last = first_window_reaching(r1)
        has = r0 < cnt
        return jnp.where(has, w_first, 0), jnp.where(has, w_last - w_first + 1, 0)

    def produce():
        def exhausted(c):
            return (c[0] < nb) & (c[1] >= c[3])

        def next_step(c):
            s = c[0] + 1
            w_first, count = step_range(jnp.minimum(s, nb - 1))
            return s, jnp.zeros_like(c[1]), w_first, count

        ps, pk, pw, pn = lax.while_loop(exhausted, next_step,
                                        (st_ref[0], st_ref[1], st_ref[4], st_ref[5]))
        st_ref[0] = ps
        st_ref[4] = pw
        st_ref[5] = pn

        @pl.when(ps < nb)
        def _():
            issued = st_ref[2]
            copy(pw + pk, issued % nslot).start()
            st_ref[1] = pk + 1
            st_ref[2] = issued + 1

        @pl.when(ps >= nb)
        def _():
            st_ref[1] = pk

    w0, nw = step_range(b)

    @pl.when(b == 0)
    def _():
        for idx in range(4):
            st_ref[idx] = 0
        st_ref[4] = w0
        st_ref[5] = nw
        for _ in range(DSP_DEPTH):
            produce()

    g0 = st_ref[3]
    acc_ref[...] = jnp.zeros(acc_ref.shape, F32)
    gacc_ref[...] = jnp.zeros(gacc_ref.shape, F32)
    slot_id = b * tm + lax.broadcasted_iota(I32, (tm, win), 0)

    def body(k, carry):
        slot = (g0 + k) % nslot
        copy(w0 + k, slot).wait()
        produce()
        pos = pos_ref[w0 + k]
        gt = gt_ref[w0 + k]
        eq0 = pos[0:1, :] == slot_id
        eq1 = pos[1:2, :] == slot_id
        onehot = jnp.where(eq0, 1.0, jnp.where(eq1, 1.0, 0.0)).astype(BF16)
        acc_ref[...] += _dot(onehot, buf_ref[slot])
        gsel = jnp.where(eq0, gt[0:1, :], 0.0) + jnp.where(eq1, gt[1:2, :], 0.0)
        gacc_ref[...] += jnp.sum(gsel, axis=-1, keepdims=True)
        return carry

    lax.fori_loop(0, nw, body, 0)
    st_ref[3] = g0 + nw
    xb_ref[...] = acc_ref[...].astype(BF16)
    gate_ref[...] = gacc_ref[...]


def _dispatch(meta, wcum_flat, h, pos_w, gate_w, n_steps):
    n, d = h.shape
    tm, win = DSP_TM, DSP_WIN
    nwt = n // win
    return pl.pallas_call(
        _dispatch_kernel,
        out_shape=(jax.ShapeDtypeStruct((n_steps * tm, d), BF16),
                   jax.ShapeDtypeStruct((n_steps * tm, 1), F32)),
        grid_spec=pltpu.PrefetchScalarGridSpec(
            num_scalar_prefetch=2,
            grid=(n_steps,),
            in_specs=[
                pl.BlockSpec(memory_space=pl.ANY),
                pl.BlockSpec((nwt, 2, win), lambda b, w0, nw: (0, 0, 0)),
                pl.BlockSpec((nwt, 2, win), lambda b, w0, nw: (0, 0, 0)),
            ],
            out_specs=(pl.BlockSpec((tm, d), lambda b, w0, nw: (b, 0)),
                       pl.BlockSpec((tm, 1), lambda b, w0, nw: (b, 0))),
            scratch_shapes=[
                pltpu.VMEM((DSP_DEPTH + 1, win, d), BF16),
                pltpu.SemaphoreType.DMA((DSP_DEPTH + 1,)),
                pltpu.VMEM((tm, d), F32),
                pltpu.VMEM((tm, 1), F32),
                pltpu.SMEM((6,), I32),
            ],
        ),
        compiler_params=_params(("arbitrary",)),
        name="moe_dispatch",
    )(meta, wcum_flat, h, pos_w, gate_w)


def _expert_kernel(be_ref, used_ref, xb_ref, gate_ref, w1_ref, w3_ref, w2_ref, yb_ref, acc_ref):
    b = pl.program_id(0)
    j = pl.program_id(1)
    last = pl.num_programs(1) - 1
    live = b < used_ref[0]

    @pl.when((b == 0) & (j == 0))
    def _():
        acc_ref[...] = jnp.zeros(acc_ref.shape, F32)

    @pl.when(live)
    def _():
        rows = xb_ref.shape[0] // ROW_SPLIT
        for r in range(ROW_SPLIT):
            sl = slice(r * rows, (r + 1) * rows)
            xb = xb_ref[sl, :]
            t = (_silu(_dot(xb, w1_ref[...])) * _dot(xb, w3_ref[...])).astype(BF16)
            part = _dot(t, w2_ref[...])
            acc_ref[sl, :] = jnp.where(j == 0, 0.0, acc_ref[sl, :]) + part

        @pl.when(j == last)
        def _():
            yb_ref[...] = (acc_ref[...] * gate_ref[...]).astype(BF16)

    @pl.when(jnp.logical_not(live) & (j == last))
    def _():
        yb_ref[...] = jnp.zeros(yb_ref.shape, BF16)


def _experts(block_expert, used, xb, gate_buf, w1, w3, w2, n_blocks):
    d = xb.shape[1]
    dff = w1.shape[2]
    tm = MOE_TM
    tf = dff // 2 if (dff // 2) % LANES == 0 else dff
    nj = dff // tf

    def jj(b, j, be, used):
        return jnp.where(b < used[0], j, nj - 1)

    return pl.pallas_call(
        _expert_kernel,
        out_shape=jax.ShapeDtypeStruct((n_blocks * tm, d), BF16),
        grid_spec=pltpu.PrefetchScalarGridSpec(
            num_scalar_prefetch=2,
            grid=(n_blocks, nj),
            in_specs=[
                pl.BlockSpec((tm, d), lambda b, j, be, used: (b, 0)),
                pl.BlockSpec((tm, 1), lambda b, j, be, used: (b, 0)),
                pl.BlockSpec((None, d, tf), lambda b, j, be, used: (be[b], 0, jj(b, j, be, used))),
                pl.BlockSpec((None, d, tf), lambda b, j, be, used: (be[b], 0, jj(b, j, be, used))),
                pl.BlockSpec((None, tf, d), lambda b, j, be, used: (be[b], jj(b, j, be, used), 0)),
            ],
            out_specs=pl.BlockSpec((tm, d), lambda b, j, be, used: (b, 0)),
            scratch_shapes=[pltpu.VMEM((tm, d), F32)],
        ),
        compiler_params=_params(("arbitrary", "arbitrary")),
        name="moe_experts",
    )(block_expert, used, xb, gate_buf, w1, w3, w2)


def _combine_kernel(ws_ref, nsub_ref, yb_hbm, pos_ref, x_ref, mod_ref, g_ref, o_ref,
                    buf_ref, sem_ref, xbuf_ref, xsem_ref, y_ref):
    i = pl.program_id(0)
    n_steps = pl.num_programs(0)
    ne = N_EXPERTS
    t, cw = CMB_T, CMB_WIN

    def start_row(step, e, sub):
        return pl.multiple_of(ws_ref[step * ne + e] + sub * cw, BF16_ROWS)

    def copy(step, e, slot):
        return pltpu.make_async_copy(yb_hbm.at[pl.ds(start_row(step, e, 0), cw), :],
                                     buf_ref.at[slot, e], sem_ref.at[slot, e])

    @pl.when(i == 0)
    def _():
        for e in range(ne):
            copy(0, e, 0).start()

    slot = i % 2

    @pl.when(i + 1 < n_steps)
    def _():
        for e in range(ne):
            copy(i + 1, e, 1 - slot).start()

    pe = pos_ref[...]
    pos = pe[:, 0:TOP_K]
    top = pe[:, TOP_K:2 * TOP_K]
    col = lax.broadcasted_iota(I32, (t, cw), 1)

    def onehot(rel):
        return jnp.where(rel[:, 0:1] == col, 1.0, jnp.where(rel[:, 1:2] == col, 1.0, 0.0)).astype(BF16)

    rels = []
    for e in range(ne):
        copy(i, e, slot).wait()
        rels.append(jnp.where(top == e, pos - ws_ref[i * ne + e], -1))
    picks = jnp.concatenate([onehot(rel) for rel in rels], axis=1)
    y_ref[...] = _dot(picks, buf_ref[slot].reshape(ne * cw, buf_ref.shape[-1]))

    for e in range(ne):
        nsub = nsub_ref[i * ne + e]
        for sub in range(1, CMB_MAX_SUB):
            @pl.when(nsub > sub)
            def _():
                extra = pltpu.make_async_copy(yb_hbm.at[pl.ds(start_row(i, e, sub), cw), :],
                                              xbuf_ref, xsem_ref)
                extra.start()
                extra.wait()
                y_ref[...] += _dot(onehot(rels[e] - sub * cw), xbuf_ref[...])

    o_ref[...] = x_ref[...] + mod_ref[5:6, :] * _rms(y_ref[...], g_ref[...])


def _combine(win_start, nsub, yb, pos, x2, mod_l, g, rows_per_batch):
    n, d = x2.shape
    t = CMB_T
    bpb = rows_per_batch // t
    return pl.pallas_call(
        _combine_kernel,
        out_shape=jax.ShapeDtypeStruct((n, d), F32),
        grid_spec=pltpu.PrefetchScalarGridSpec(
            num_scalar_prefetch=2,
            grid=(n // t,),
            in_specs=[
                pl.BlockSpec(memory_space=pl.ANY),
                pl.BlockSpec((t, 2 * TOP_K), lambda i, ws, ns: (i, 0)),
                pl.BlockSpec((t, d), lambda i, ws, ns: (i, 0)),
                pl.BlockSpec((None, 6, d), lambda i, ws, ns: (i // bpb, 0, 0)),
                pl.BlockSpec((1, d), lambda i, ws, ns: (0, 0)),
            ],
            out_specs=pl.BlockSpec((t, d), lambda i, ws, ns: (i, 0)),
            scratch_shapes=[
                pltpu.VMEM((2, N_EXPERTS, CMB_WIN, d), BF16),
                pltpu.SemaphoreType.DMA((2, N_EXPERTS)),
                pltpu.VMEM((CMB_WIN, d), BF16),
                pltpu.SemaphoreType.DMA(()),
                pltpu.VMEM((t, d), F32),
            ],
        ),
        compiler_params=_params(("arbitrary",)),
        name="moe_combine",
    )(win_start, nsub, yb, pos, x2, mod_l, g)


def _moe(x2, attn, sc, w_out, mod_l, gmix, gpre, gpost, w_router, w1, w3, w2, rows_per_batch):
    n, d = x2.shape
    ne, tm, dtm, win, t = N_EXPERTS, MOE_TM, DSP_TM, DSP_WIN, CMB_T
    wr_hi, wr_lo = _split2(w_router)
    wr_packed = jnp.zeros((d, LANES), BF16).at[:, 0:ne].set(wr_hi).at[:, ne:2 * ne].set(wr_lo)
    x2, h, route, wcum = _router(x2, attn, sc, w_out, mod_l, gmix, gpre, wr_packed, rows_per_batch)

    top = route[:, 0:TOP_K].astype(I32)
    gates = route[:, TOP_K:2 * TOP_K]
    rank = route[:, 2 * TOP_K:3 * TOP_K].astype(I32)
    wc = wcum.reshape(n // win, LANES)[:, 0:ne].astype(I32)
    counts = wc[-1]
    padded = (counts + tm - 1) // tm * tm
    pend = jnp.cumsum(padded)
    pstart = pend - padded
    experts = jnp.arange(ne, dtype=I32)
    pos = rank + jnp.sum(jnp.where(top[:, :, None] == experts, pstart, 0), axis=-1)

    n_blocks = (n * TOP_K) // tm + ne + 1
    blk = jnp.arange(n_blocks, dtype=I32) * tm
    block_expert = jnp.minimum(jnp.sum((pend[None, :] <= blk[:, None]).astype(I32), axis=1), ne - 1)
    used = (pend[-1] // tm).astype(I32).reshape(1)

    n_steps = n_blocks * (tm // dtm)
    meta = jnp.concatenate([pstart, counts, pend]).astype(I32)
    pos_w = pos.T.reshape(TOP_K, n // win, win).transpose(1, 0, 2)
    gate_w = gates.T.reshape(TOP_K, n // win, win).transpose(1, 0, 2)
    xb, gate_buf = _dispatch(meta, wc.T.reshape(-1), h, pos_w, gate_w, n_steps)
    yb = _experts(block_expert, used, xb, gate_buf, w1, w3, w2, n_blocks)

    at_block_end = wc[t // win - 1::t // win]
    before_block = jnp.concatenate([jnp.zeros((1, ne), I32), at_block_end[:-1]], axis=0)
    base_b = pstart[None, :] + before_block
    end_b = pstart[None, :] + at_block_end
    win_start = base_b // BF16_ROWS * BF16_ROWS
    nsub = jnp.where(end_b > base_b, (end_b - win_start + CMB_WIN - 1) // CMB_WIN, 0)
    return _combine(win_start.astype(I32).reshape(-1), nsub.astype(I32).reshape(-1), yb,
                    jnp.concatenate([pos, top], axis=1), x2, mod_l, gpost, rows_per_batch)


def _reorder_w_in(w):
    main = jnp.concatenate([w[:, 0:3072], w[:, 3080:4616]], axis=1).astype(BF16)
    wdt = jnp.zeros((w.shape[0], LANES), BF16).at[:, 0:SSD_HEADS].set(w[:, 3072:3080].astype(BF16))
    return main, wdt


def _pad_row(v, width=LANES):
    return jnp.zeros((1, width), F32).at[0, 0:v.shape[0]].set(v)


def kernel(x, c, w_mod, b_mod, norm_mix_pre, norm_mix_post, norm_ffn_pre, norm_ffn_post, w_in, w_out, lambda_qk, attn_subln, ssd_conv_w, ssd_conv_b, ssd_dt_bias, ssd_a_log, ssd_d, ssd_norm, sconv_w, ffn_w1, ffn_w3, ffn_w2, moe_router, moe_w1, moe_w3, moe_w2):
    batch, seq, d = x.shape
    depth = w_mod.shape[0]
    n = batch * seq
    assert seq % SSD_CHUNK == 0 and w_in.shape[2] == PROJ_COLS + SSD_HEADS
    mod = _modulation(c, w_mod, b_mod).reshape(depth, batch, 6, d)
    x2 = x.reshape(n, d)
    for i in range(depth):
        mod_l = mod[i]
        w_main, w_dt = _reorder_w_in(w_in[i])
        proj, dt_raw = _inproj(x2, mod_l, norm_mix_pre[i][None, :], w_main, w_dt, seq)
        lam_init = 0.8 - 0.6 * math.exp(-0.3 * i)
        attn = _attention(proj, lambda_qk[i], attn_subln[i][None, :], batch, seq, lam_init)
        sc = _ssd_sconv(proj, dt_raw, ssd_conv_w[i], ssd_conv_b[i][None, :], _pad_row(ssd_dt_bias[i]),
                        _pad_row(ssd_a_log[i]), jnp.repeat(ssd_d[i], SSD_HEAD_DIM)[None, :],
                        ssd_norm[i][None, :], sconv_w[i], batch, seq)
        mix = (x2, attn, sc, w_out[i].astype(BF16), mod_l, norm_mix_post[i][None, :])
        if i % 2 == 0:
            x2 = _ffn(*mix, norm_ffn_pre[i][None, :], norm_ffn_post[i][None, :],
                      ffn_w1[i // 2].astype(BF16), ffn_w3[i // 2].astype(BF16), ffn_w2[i // 2].astype(BF16), seq)
        else:
            x2 = _moe(*mix, norm_ffn_pre[i][None, :], norm_ffn_post[i][None, :], moe_router[i // 2],
                      moe_w1[i // 2].astype(BF16), moe_w3[i // 2].astype(BF16), moe_w2[i // 2].astype(BF16), seq)
    return x2.reshape(batch, seq, d)
```

```python
import functools
import math

import jax
import jax.numpy as jnp
from jax import lax
from jax.experimental import pallas as pl
from jax.experimental.pallas import tpu as pltpu

F32 = jnp.float32
BF16 = jnp.bfloat16
I32 = jnp.int32

EPS = 1e-6
GROUP_WIDTH = 512
ATTN_HEADS = 4
ATTN_HEAD_DIM = 64
SSD_HEAD_DIM = 64
SSD_HEADS = 8
SSD_GROUPS = 2
SSD_STATE = 128
SSD_CONV = 4
SSD_CHUNK = 128
SSD_XBC = GROUP_WIDTH + 2 * SSD_GROUPS * SSD_STATE
SC_CONV = 3
N_EXPERTS = 8
TOP_K = 2

LANES = 128
SUBLANES = 8
BF16_ROWS = 16
VMEM_LIMIT = 48 * 1024 * 1024
NEG_BIG = -1e30
ROW_SPLIT = 2
RESIDENT = dict(pipeline_mode=pl.Buffered(1))

COL_Q, COL_K, COL_V, COL_Z, COL_XBC, COL_GB, COL_GC, COL_U = 0, 512, 1024, 1536, 2048, 3072, 3584, 4096
PROJ_COLS = 4608

NT_DIMS = (((1,), (1,)), ((), ()))


def _dot(a, b):
    return jnp.dot(a, b, preferred_element_type=F32)


def _dot_nt(a, b):
    return lax.dot_general(a, b, NT_DIMS, preferred_element_type=F32)


def _split2(x):
    hi = x.astype(BF16)
    lo = (x - hi.astype(F32)).astype(BF16)
    return hi, lo


def _split3(x):
    hi = x.astype(BF16)
    r = x - hi.astype(F32)
    mid = r.astype(BF16)
    lo = (r - mid.astype(F32)).astype(BF16)
    return hi, mid, lo


def _sigmoid(x):
    return 0.5 * jnp.tanh(0.5 * x) + 0.5


def _silu(x):
    return x * _sigmoid(x)


def _softplus(x):
    return jnp.maximum(x, 0.0) + jnp.log(1.0 + jnp.exp(-jnp.abs(x)))


def _rms(x, g):
    return x * lax.rsqrt(jnp.mean(x * x, axis=-1, keepdims=True) + EPS) * g


def _params(sem):
    return pltpu.CompilerParams(dimension_semantics=sem, vmem_limit_bytes=VMEM_LIMIT)


def _pick(n, pref):
    t = min(n, pref)
    assert n % t == 0, (n, pref)
    return t


def _mod_kernel(c_ref, w_ref, b_ref, o_ref):
    c = c_ref[...]
    ah, al = _split2(_silu(c))
    wh, wl = _split2(w_ref[...])
    o_ref[...] = _dot(ah, wh) + _dot(ah, wl) + _dot(al, wh) + b_ref[...]


def _modulation(c, w_mod, b_mod):
    depth, d, n6 = w_mod.shape
    b = c.shape[0]
    tn = _pick(n6, 1536)
    return pl.pallas_call(
        _mod_kernel,
        out_shape=jax.ShapeDtypeStruct((depth, b, n6), F32),
        grid=(depth, n6 // tn),
        in_specs=[
            pl.BlockSpec((b, d), lambda l, j: (0, 0)),
            pl.BlockSpec((None, d, tn), lambda l, j: (l, 0, j)),
            pl.BlockSpec((None, 1, tn), lambda l, j: (l, 0, j)),
        ],
        out_specs=pl.BlockSpec((None, b, tn), lambda l, j: (l, 0, j)),
        compiler_params=_params(("parallel", "parallel")),
        name="modulation",
    )(c, w_mod, b_mod.reshape(depth, 1, n6))


def _inproj_kernel(x_ref, mod_ref, g_ref, w_ref, wdt_ref, o_ref, dt_ref):
    rows = x_ref.shape[0] // ROW_SPLIT
    for r in range(ROW_SPLIT):
        sl = slice(r * rows, (r + 1) * rows)
        h = _rms(x_ref[sl, :], g_ref[...]) * (1.0 + mod_ref[1:2, :]) + mod_ref[0:1, :]
        hb = h.astype(BF16)
        dt_ref[sl, :] = _dot(hb, wdt_ref[...])
        o_ref[sl, :] = _dot(hb, w_ref[...]).astype(BF16)


def _inproj(x2, mod_l, g, w, wdt, rows_per_batch):
    n, d = x2.shape
    tm = _pick(rows_per_batch, 512)
    bpb = rows_per_batch // tm
    return pl.pallas_call(
        _inproj_kernel,
        out_shape=(jax.ShapeDtypeStruct((n, PROJ_COLS), BF16),
                   jax.ShapeDtypeStruct((n, LANES), F32)),
        grid=(n // tm,),
        in_specs=[
            pl.BlockSpec((tm, d), lambda i: (i, 0)),
            pl.BlockSpec((None, 6, d), lambda i: (i // bpb, 0, 0)),
            pl.BlockSpec((1, d), lambda i: (0, 0)),
            pl.BlockSpec((d, PROJ_COLS), lambda i: (0, 0), **RESIDENT),
            pl.BlockSpec((d, LANES), lambda i: (0, 0), **RESIDENT),
        ],
        out_specs=(pl.BlockSpec((tm, PROJ_COLS), lambda i: (i, 0)),
                   pl.BlockSpec((tm, LANES), lambda i: (i, 0))),
        compiler_params=_params(("parallel",)),
        name="inproj",
    )(x2, mod_l, g, w, wdt)


ATTN_GROUP = 4


def _attn_kernel(q_ref, k_ref, v_ref, lq_ref, sub_ref, o_ref, qt_ref, vt_ref, m_ref, acc_ref,
                 s_ref, cmax_ref, *, tq, lam_init):
    i = pl.program_id(2)
    d = ATTN_HEAD_DIM
    n_chunks = vt_ref.shape[1]
    heads = range(ATTN_GROUP)

    def lanes(hd):
        return slice(hd * LANES, (hd + 1) * LANES)

    @pl.when(i == 0)
    def _():
        ones_row = jnp.where(lax.broadcasted_iota(I32, (BF16_ROWS, tq), 0) == 0, 1.0, 0.0).astype(BF16)
        for hd in heads:
            for cidx in range(n_chunks):
                vt = v_ref[cidx * tq:(cidx + 1) * tq, lanes(hd)].astype(F32).T.astype(BF16)
                vt_ref[hd, cidx, 0:LANES, :] = vt
                vt_ref[hd, cidx, LANES:, :] = ones_row

    for hd in heads:
        qt = (q_ref[:, lanes(hd)].astype(F32) * (d ** -0.5 * math.log2(math.e))).T
        row = lax.broadcasted_iota(I32, qt.shape, 0)
        qt_ref[hd, :, 0:tq] = jnp.where(row < d, qt, 0.0).astype(BF16)
        qt_ref[hd, :, tq:] = jnp.where(row >= d, qt, 0.0).astype(BF16)
    m_ref[...] = jnp.full(m_ref.shape, NEG_BIG, F32)
    acc_ref[...] = jnp.zeros(acc_ref.shape, F32)

    def scores(hd, j, masked):
        start = pl.multiple_of(j * tq, tq)
        s = _dot(k_ref[pl.ds(start, tq), lanes(hd)], qt_ref[hd])
        if masked:
            kk = lax.broadcasted_iota(I32, (tq, tq), 0)
            qq = lax.broadcasted_iota(I32, (tq, tq), 1)
            keep = kk <= qq
            s = jnp.where(jnp.concatenate([keep, keep], axis=1), s, NEG_BIG)
        s_ref[hd] = s
        cmax_ref[hd] = jnp.max(s, axis=0, keepdims=True)

    def accumulate(hd, j):
        m_prev = m_ref[hd]
        m_new = jnp.maximum(m_prev, cmax_ref[hd])
        alpha = jnp.exp2(m_prev - m_new)
        p = jnp.exp2(s_ref[hd] - m_new)
        acc_ref[hd] = alpha * acc_ref[hd] + _dot(vt_ref[hd, j], p.astype(BF16))
        m_ref[hd] = m_new

    last = ATTN_GROUP - 1
    scores(0, i, True)
    for hd in range(1, ATTN_GROUP):
        scores(hd, i, True)
        accumulate(hd - 1, i)

    def body(j, carry):
        scores(0, j, False)
        accumulate(last, jnp.where(j == 0, i, j - 1))
        for hd in range(1, ATTN_GROUP):
            scores(hd, j, False)
            accumulate(hd - 1, j)
        return carry

    lax.fori_loop(0, i, body, 0)
    accumulate(last, jnp.where(i == 0, i, i - 1))

    lq = lq_ref[...]
    lam = (jnp.exp(jnp.sum(lq[0:1, :] * lq[1:2, :], axis=-1, keepdims=True))
           - jnp.exp(jnp.sum(lq[2:3, :] * lq[3:4, :], axis=-1, keepdims=True)) + lam_init)
    for hd in heads:
        inv = 1.0 / acc_ref[hd, LANES:LANES + 1, :]
        ot = (acc_ref[hd, 0:LANES, 0:tq] * inv[:, 0:tq]
              - lam * (acc_ref[hd, 0:LANES, tq:] * inv[:, tq:]))
        ot = ot * lax.rsqrt(jnp.mean(ot * ot, axis=0, keepdims=True) + EPS)
        o_ref[:, lanes(hd)] = (ot.T * sub_ref[...] * (1.0 - lam_init)).astype(BF16)


def _attention(proj, lambda_qk, subln, batch, seq, lam_init):
    n = proj.shape[0]
    tq = _pick(seq, 512)
    nq = seq // tq
    hp = ATTN_GROUP
    wide = hp * LANES
    assert ATTN_HEADS % hp == 0
    return pl.pallas_call(
        functools.partial(_attn_kernel, tq=tq, lam_init=lam_init),
        out_shape=jax.ShapeDtypeStruct((n, GROUP_WIDTH), BF16),
        grid=(batch, ATTN_HEADS // hp, nq),
        in_specs=[
            pl.BlockSpec((tq, wide), lambda b, hh, i: (b * nq + i, COL_Q // wide + hh)),
            pl.BlockSpec((seq, wide), lambda b, hh, i: (b, COL_K // wide + hh), **RESIDENT),
            pl.BlockSpec((seq, wide), lambda b, hh, i: (b, COL_V // wide + hh), **RESIDENT),
            pl.BlockSpec((4, ATTN_HEAD_DIM), lambda b, hh, i: (0, 0)),
            pl.BlockSpec((1, LANES), lambda b, hh, i: (0, 0)),
        ],
        out_specs=pl.BlockSpec((tq, wide), lambda b, hh, i: (b * nq + i, hh)),
        scratch_shapes=[
            pltpu.VMEM((hp, LANES, 2 * tq), BF16),
            pltpu.VMEM((hp, nq, LANES + BF16_ROWS, tq), BF16),
            pltpu.VMEM((hp, 1, 2 * tq), F32),
            pltpu.VMEM((hp, LANES + BF16_ROWS, 2 * tq), F32),
            pltpu.VMEM((hp, tq, 2 * tq), F32),
            pltpu.VMEM((hp, 1, 2 * tq), F32),
        ],
        compiler_params=_params(("parallel", "parallel", "arbitrary")),
        name="diff_attention",
    )(proj, proj, proj, lambda_qk, subln)


def _causal_conv(x, carry, w, width):
    row = lax.broadcasted_iota(I32, carry.shape, 0)
    out = x * w[width - 1:width, :]
    for k in range(1, width):
        xr = pltpu.roll(x, k, 0)
        cr = pltpu.roll(carry, k, 0)
        head = jnp.where(row < k, cr, xr[0:SUBLANES, :])
        xk = jnp.concatenate([head, xr[SUBLANES:, :]], axis=0)
        out = out + xk * w[width - 1 - k:width - k, :]
    return out


def _ssd_kernel(z_ref, xbc_ref, gb_ref, gc_ref, u_ref, dt_ref, cw_ref, cb_ref, dtb_ref, alog_ref,
                dskip_ref, ng_ref, sw_ref, o_ref,
                cx_ref, cs_ref, state_ref, xs_ref, bm_ref, cm_ref, *, tc):
    L = SSD_CHUNK
    gw = GROUP_WIDTH
    ns = SSD_STATE

    @pl.when(pl.program_id(1) == 0)
    def _():
        cx_ref[...] = jnp.zeros(cx_ref.shape, F32)
        cs_ref[...] = jnp.zeros(cs_ref.shape, F32)
        state_ref[...] = jnp.zeros(state_ref.shape, F32)

    pu = gc_ref[...].astype(F32) * u_ref[...].astype(F32)
    sconv = gb_ref[...].astype(F32) * _causal_conv(pu, cs_ref[...], sw_ref[...], SC_CONV)
    cs_ref[...] = pu[tc - SUBLANES:, :]
    o_ref[:, gw:] = sconv.astype(BF16)

    xbc = xbc_ref[...].astype(F32)
    act = _silu(_causal_conv(xbc, cx_ref[...], cw_ref[...], SSD_CONV) + cb_ref[...])
    cx_ref[...] = xbc[tc - SUBLANES:, :]
    xs_ref[...] = act[:, 0:gw]
    bm_ref[...] = act[:, gw:gw + SSD_GROUPS * ns]
    cm_ref[...] = act[:, gw + SSD_GROUPS * ns:]

    lane = lax.broadcasted_iota(I32, (L, LANES), 1)
    lo = lane < SSD_HEAD_DIM
    rr = lax.broadcasted_iota(I32, (L, L), 0)
    cc = lax.broadcasted_iota(I32, (L, L), 1)
    tril = cc <= rr
    ltri = jnp.where(tril, 1.0, 0.0).astype(BF16)
    head_lane = lane < SSD_HEADS
    a_row = -jnp.exp(alog_ref[...])

    def pair_pattern(mat, h0):
        lo_b = lo[0:mat.shape[0], :]
        return jnp.where(lo_b, mat[:, h0:h0 + 1], mat[:, h0 + 1:h0 + 2])

    def chunk_body(c, carry):
        rows = slice(c * L, (c + 1) * L)
        dt = _softplus(dt_ref[rows, :] + dtb_ref[...])
        da = jnp.where(head_lane, dt * a_row, 0.0)
        d_hi, d_mid, d_lo = _split3(da)
        acum = _dot(ltri, d_hi) + _dot(ltri, d_mid) + _dot(ltri, d_lo)
        acum_t = acum.T
        a_last = acum[L - 1:L, :]
        ys = []
        for g in range(SSD_GROUPS):
            bg = bm_ref[rows, g * ns:(g + 1) * ns]
            cg = cm_ref[rows, g * ns:(g + 1) * ns].astype(BF16)
            cb = _dot_nt(cg, bg.astype(BF16))
            bg_t = bg.T.astype(BF16)
            for jp in range(SSD_HEADS // SSD_GROUPS // 2):
                j = g * (SSD_HEADS // SSD_GROUPS // 2) + jp
                h0 = 2 * j
                xs = xs_ref[rows, j * LANES:(j + 1) * LANES]
                xdt = xs * pair_pattern(dt, h0)
                cols = [jnp.broadcast_to(acum[:, hh:hh + 1], (L, LANES)) for hh in (h0, h0 + 1)]
                acum_pat = jnp.where(lo, cols[0], cols[1])
                alast_pat = pair_pattern(a_last, h0)
                y = None
                for hh, col, keep in ((h0, cols[0], lo), (h0 + 1, cols[1], jnp.logical_not(lo))):
                    seg = col - acum_t[hh:hh + 1, :]
                    dec = jnp.exp(jnp.where(tril, seg, NEG_BIG))
                    mm = (dec * cb).astype(BF16)
                    part = _dot(mm, jnp.where(keep, xdt, 0.0).astype(BF16))
                    y = part if y is None else y + part
                st = state_ref[j]
                y = y + _dot(cg, st.astype(BF16)) * jnp.exp(acum_pat)
                dte = jnp.exp(alast_pat - acum_pat)
                contrib = _dot(bg_t, (xdt * dte).astype(BF16))
                state_ref[j] = st * jnp.exp(alast_pat) + contrib
                ys.append(y + dskip_ref[:, j * LANES:(j + 1) * LANES] * xs)
        yv = jnp.concatenate(ys, axis=-1)
        z = z_ref[rows, :].astype(F32)
        vv = yv * _silu(z)
        gwid = gw // SSD_GROUPS
        outs = []
        for g in range(SSD_GROUPS):
            vg = vv[:, g * gwid:(g + 1) * gwid]
            outs.append(vg * lax.rsqrt(jnp.mean(vg * vg, axis=-1, keepdims=True) + EPS))
        o_ref[rows, 0:gw] = (jnp.concatenate(outs, axis=-1) * ng_ref[...]).astype(BF16)
        return carry

    for c in range(tc // L):
        chunk_body(c, 0)


def _ssd_sconv(proj, dt_raw, cw, cb, dtb, alog, dskip, ng, sw, batch, seq):
    n = proj.shape[0]
    tc = _pick(seq, 512)
    nt = seq // tc
    row = lambda b, t: b * nt + t
    const = lambda b, t: (0, 0)
    return pl.pallas_call(
        functools.partial(_ssd_kernel, tc=tc),
        out_shape=jax.ShapeDtypeStruct((n, 2 * GROUP_WIDTH), BF16),
        grid=(batch, nt),
        in_specs=[
            pl.BlockSpec((tc, GROUP_WIDTH), lambda b, t: (row(b, t), COL_Z // GROUP_WIDTH)),
            pl.BlockSpec((tc, SSD_XBC), lambda b, t: (row(b, t), COL_XBC // SSD_XBC)),
            pl.BlockSpec((tc, GROUP_WIDTH), lambda b, t: (row(b, t), COL_GB // GROUP_WIDTH)),
            pl.BlockSpec((tc, GROUP_WIDTH), lambda b, t: (row(b, t), COL_GC // GROUP_WIDTH)),
            pl.BlockSpec((tc, GROUP_WIDTH), lambda b, t: (row(b, t), COL_U // GROUP_WIDTH)),
            pl.BlockSpec((tc, LANES), lambda b, t: (row(b, t), 0)),
            pl.BlockSpec((SSD_CONV, SSD_XBC), const),
            pl.BlockSpec((1, SSD_XBC), const),
            pl.BlockSpec((1, LANES), const),
            pl.BlockSpec((1, LANES), const),
            pl.BlockSpec((1, GROUP_WIDTH), const),
            pl.BlockSpec((1, GROUP_WIDTH), const),
            pl.BlockSpec((SC_CONV, GROUP_WIDTH), const),
        ],
        out_specs=pl.BlockSpec((tc, 2 * GROUP_WIDTH), lambda b, t: (row(b, t), 0)),
        scratch_shapes=[
            pltpu.VMEM((SUBLANES, SSD_XBC), F32),
            pltpu.VMEM((SUBLANES, GROUP_WIDTH), F32),
            pltpu.VMEM((SSD_HEADS // 2, SSD_STATE, LANES), F32),
            pltpu.VMEM((tc, GROUP_WIDTH), F32),
            pltpu.VMEM((tc, SSD_GROUPS * SSD_STATE), F32),
            pltpu.VMEM((tc, SSD_GROUPS * SSD_STATE), F32),
        ],
        compiler_params=_params(("parallel", "arbitrary")),
        name="ssd_sconv",
    )(proj, proj, proj, proj, proj, dt_raw, cw, cb, dtb, alog, dskip, ng, sw)


def _mixer_residual(x, attn, sc, wo_ref, mod_ref, gmix_ref):
    gw = GROUP_WIDTH
    y = _dot(attn, wo_ref[0:gw, :]) + _dot(sc, wo_ref[gw:, :])
    return x + mod_ref[2:3, :] * _rms(y, gmix_ref[...])


def _mixer_specs(tm, d, bpb):
    return [
        pl.BlockSpec((tm, d), lambda i: (i, 0)),
        pl.BlockSpec((tm, GROUP_WIDTH), lambda i: (i, 0)),
        pl.BlockSpec((tm, 2 * GROUP_WIDTH), lambda i: (i, 0)),
        pl.BlockSpec((3 * GROUP_WIDTH, d), lambda i: (0, 0), **RESIDENT),
        pl.BlockSpec((None, 6, d), lambda i: (i // bpb, 0, 0)),
        pl.BlockSpec((1, d), lambda i: (0, 0)),
    ]


def _ffn_kernel(x_ref, a_ref, sc_ref, wo_ref, mod_ref, gmix_ref, gpre_ref, gpost_ref,
                w1_ref, w3_ref, w2_ref, o_ref):
    rows = x_ref.shape[0] // ROW_SPLIT
    for r in range(ROW_SPLIT):
        sl = slice(r * rows, (r + 1) * rows)
        x = _mixer_residual(x_ref[sl, :], a_ref[sl, :], sc_ref[sl, :], wo_ref, mod_ref, gmix_ref)
        h = (_rms(x, gpre_ref[...]) * (1.0 + mod_ref[4:5, :]) + mod_ref[3:4, :]).astype(BF16)
        t = (_silu(_dot(h, w1_ref[...])) * _dot(h, w3_ref[...])).astype(BF16)
        y = _dot(t, w2_ref[...])
        o_ref[sl, :] = x + mod_ref[5:6, :] * _rms(y, gpost_ref[...])


def _ffn(x2, attn, sc, w_out, mod_l, gmix, gpre, gpost, w1, w3, w2, rows_per_batch):
    n, d = x2.shape
    dff = w1.shape[1]
    tm = _pick(rows_per_batch, 512)
    bpb = rows_per_batch // tm
    return pl.pallas_call(
        _ffn_kernel,
        out_shape=jax.ShapeDtypeStruct((n, d), F32),
        grid=(n // tm,),
        in_specs=_mixer_specs(tm, d, bpb) + [
            pl.BlockSpec((1, d), lambda i: (0, 0)),
            pl.BlockSpec((1, d), lambda i: (0, 0)),
            pl.BlockSpec((d, dff), lambda i: (0, 0), **RESIDENT),
            pl.BlockSpec((d, dff), lambda i: (0, 0), **RESIDENT),
            pl.BlockSpec((dff, d), lambda i: (0, 0), **RESIDENT),
        ],
        out_specs=pl.BlockSpec((tm, d), lambda i: (i, 0)),
        compiler_params=_params(("parallel",)),
        name="dense_ffn",
    )(x2, attn, sc, w_out, mod_l, gmix, gpre, gpost, w1, w3, w2)


MOE_TM = 512
DSP_TM = 256
DSP_WIN = 256
DSP_DEPTH = 6
CMB_T = 512
CMB_WIN = 256
CMB_MAX_SUB = (CMB_T + BF16_ROWS + CMB_WIN - 1) // CMB_WIN


def _router_kernel(x_ref, a_ref, sc_ref, wo_ref, mod_ref, gmix_ref, g_ref, wr_ref,
                   xo_ref, h_ref, route_ref, wcum_ref, run_ref):
    @pl.when(pl.program_id(0) == 0)
    def _():
        run_ref[...] = jnp.zeros(run_ref.shape, F32)

    x = _mixer_residual(x_ref[...], a_ref[...], sc_ref[...], wo_ref, mod_ref, gmix_ref)
    xo_ref[...] = x
    h = _rms(x, g_ref[...]) * (1.0 + mod_ref[4:5, :]) + mod_ref[3:4, :]
    hh, hl = _split2(h)
    h_ref[...] = hh
    a = _dot(hh, wr_ref[...])
    b = _dot(hl, wr_ref[...])
    ne = N_EXPERTS
    logits = a + b + pltpu.roll(a, LANES - ne, 1)
    lane = lax.broadcasted_iota(I32, logits.shape, 1)
    valid = lane < ne
    l1 = jnp.max(jnp.where(valid, logits, NEG_BIG), axis=-1, keepdims=True)
    e1 = jnp.min(jnp.where(valid & (logits == l1), lane, LANES), axis=-1, keepdims=True)
    rest = valid & (lane != e1)
    l2 = jnp.max(jnp.where(rest, logits, NEG_BIG), axis=-1, keepdims=True)
    e2 = jnp.min(jnp.where(rest & (logits == l2), lane, LANES), axis=-1, keepdims=True)
    w = jnp.exp(l2 - l1)
    g1 = 1.0 / (1.0 + w)
    g2 = w / (1.0 + w)

    tm = logits.shape[0]
    sel = jnp.where(lane == e1, 1.0, jnp.where(lane == e2, 1.0, 0.0))
    rr = lax.broadcasted_iota(I32, (tm, tm), 0)
    cc = lax.broadcasted_iota(I32, (tm, tm), 1)
    before = jnp.where(cc < rr, 1.0, 0.0).astype(BF16)
    excl = _dot(before, sel.astype(BF16)) + run_ref[...]
    rank1 = jnp.sum(jnp.where(lane == e1, excl, 0.0), axis=-1, keepdims=True)
    rank2 = jnp.sum(jnp.where(lane == e2, excl, 0.0), axis=-1, keepdims=True)
    total = excl + sel
    for wdx in range(tm // DSP_WIN):
        wcum_ref[wdx:wdx + 1, :] = total[(wdx + 1) * DSP_WIN - 1:(wdx + 1) * DSP_WIN, :]
    run_ref[...] = total[tm - 1:tm, :]

    lane8 = lax.broadcasted_iota(I32, route_ref.shape, 1)
    cols = (e1.astype(F32), e2.astype(F32), g1, g2, rank1, rank2)
    out = jnp.zeros(route_ref.shape, F32)
    for idx, col in enumerate(cols):
        out = jnp.where(lane8 == idx, col, out)
    route_ref[...] = out


def _router(x2, attn, sc, w_out, mod_l, gmix, g, wr_packed, rows_per_batch):
    n, d = x2.shape
    tm = _pick(rows_per_batch, 512)
    assert tm % DSP_WIN == 0
    bpb = rows_per_batch // tm
    return pl.pallas_call(
        _router_kernel,
        out_shape=(jax.ShapeDtypeStruct((n, d), F32), jax.ShapeDtypeStruct((n, d), BF16),
                   jax.ShapeDtypeStruct((n, SUBLANES), F32),
                   jax.ShapeDtypeStruct((n // tm, tm // DSP_WIN, LANES), F32)),
        grid=(n // tm,),
        in_specs=_mixer_specs(tm, d, bpb) + [
            pl.BlockSpec((1, d), lambda i: (0, 0)),
            pl.BlockSpec((d, LANES), lambda i: (0, 0)),
        ],
        out_specs=(pl.BlockSpec((tm, d), lambda i: (i, 0)),
                   pl.BlockSpec((tm, d), lambda i: (i, 0)),
                   pl.BlockSpec((tm, SUBLANES), lambda i: (i, 0)),
                   pl.BlockSpec((None, tm // DSP_WIN, LANES), lambda i: (i, 0, 0))),
        scratch_shapes=[pltpu.VMEM((1, LANES), F32)],
        compiler_params=_params(("arbitrary",)),
        name="moe_router",
    )(x2, attn, sc, w_out, mod_l, gmix, g, wr_packed)


def _dispatch_kernel(meta_ref, wc_ref, h_hbm, pos_ref, gt_ref, xb_ref, gate_ref,
                     buf_ref, sem_ref, acc_ref, gacc_ref, st_ref):
    b = pl.program_id(0)
    nb = pl.num_programs(0)
    ne = N_EXPERTS
    tm, win = DSP_TM, DSP_WIN
    nslot = DSP_DEPTH + 1
    nwt = pos_ref.shape[0]

    def copy(w, slot):
        return pltpu.make_async_copy(h_hbm.at[pl.ds(pl.multiple_of(w * win, win), win), :],
                                     buf_ref.at[slot], sem_ref.at[slot])

    def step_range(s):
        p = s * tm
        e = jnp.zeros((), I32)
        for j in range(ne - 1):
            e = e + (p >= meta_ref[2 * ne + j]).astype(I32)
        r0 = p - meta_ref[e]
        cnt = meta_ref[ne + e]
        r1 = jnp.minimum(r0 + tm, cnt)

        def first_window_reaching(target):
            def halve(_, c):
                lo, hi = c
                mid = (lo + hi) // 2
                less = wc_ref[e * nwt + mid] < target
                return jnp.where(less, mid + 1, lo), jnp.where(less, hi, mid)

            return lax.fori_loop(0, max(nwt - 1, 1).bit_length(), halve,
                                 (jnp.zeros((), I32), jnp.full((), nwt - 1, I32)))[0]

        w_first = first_window_reaching(r0 + 1)
        w_last = first_window_reaching(r1)
        has = r0 < cnt
        return jnp.where(has, w_first, 0), jnp.where(has, w_last - w_first + 1, 0)

    def produce():
        def exhausted(c):
            return (c[0] < nb) & (c[1] >= c[3])

        def next_step(c):
            s = c[0] + 1
            w_first, count = step_range(jnp.minimum(s, nb - 1))
            return s, jnp.zeros_like(c[1]), w_first, count

        ps, pk, pw, pn = lax.while_loop(exhausted, next_step,
                                        (st_ref[0], st_ref[1], st_ref[4], st_ref[5]))
        st_ref[0] = ps
        st_ref[4] = pw
        st_ref[5] = pn

        @pl.when(ps < nb)
        def _():
            issued = st_ref[2]
            copy(pw + pk, issued % nslot).start()
            st_ref[1] = pk + 1
            st_ref[2] = issued + 1

        @pl.when(ps >= nb)
        def _():
            st_ref[1] = pk

    w0, nw = step_range(b)

    @pl.when(b == 0)
    def _():
        for idx in range(4):
            st_ref[idx] = 0
        st_ref[4] = w0
        st_ref[5] = nw
        for _ in range(DSP_DEPTH):
            produce()

    g0 = st_ref[3]
    acc_ref[...] = jnp.zeros(acc_ref.shape, F32)
    gacc_ref[...] = jnp.zeros(gacc_ref.shape, F32)
    slot_id = b * tm + lax.broadcasted_iota(I32, (tm, win), 0)

    def body(k, carry):
        slot = (g0 + k) % nslot
        copy(w0 + k, slot).wait()
        produce()
        pos = pos_ref[w0 + k]
        gt = gt_ref[w0 + k]
        eq0 = pos[0:1, :] == slot_id
        eq1 = pos[1:2, :] == slot_id
        onehot = jnp.where(eq0, 1.0, jnp.where(eq1, 1.0, 0.0)).astype(BF16)
        acc_ref[...] += _dot(onehot, buf_ref[slot])
        gsel = jnp.where(eq0, gt[0:1, :], 0.0) + jnp.where(eq1, gt[1:2, :], 0.0)
        gacc_ref[...] += jnp.sum(gsel, axis=-1, keepdims=True)
        return carry

    lax.fori_loop(0, nw, body, 0)
    st_ref[3] = g0 + nw
    xb_ref[...] = acc_ref[...].astype(BF16)
    gate_ref[...] = gacc_ref[...]


def _dispatch(meta, wcum_flat, h, pos_w, gate_w, n_steps):
    n, d = h.shape
    tm, win = DSP_TM, DSP_WIN
    nwt = n // win
    return pl.pallas_call(
        _dispatch_kernel,
        out_shape=(jax.ShapeDtypeStruct((n_steps * tm, d), BF16),
                   jax.ShapeDtypeStruct((n_steps * tm, 1), F32)),
        grid_spec=pltpu.PrefetchScalarGridSpec(
            num_scalar_prefetch=2,
            grid=(n_steps,),
            in_specs=[
                pl.BlockSpec(memory_space=pl.ANY),
                pl.BlockSpec((nwt, 2, win), lambda b, w0, nw: (0, 0, 0)),
                pl.BlockSpec((nwt, 2, win), lambda b, w0, nw: (0, 0, 0)),
            ],
            out_specs=(pl.BlockSpec((tm, d), lambda b, w0, nw: (b, 0)),
                       pl.BlockSpec((tm, 1), lambda b, w0, nw: (b, 0))),
            scratch_shapes=[
                pltpu.VMEM((DSP_DEPTH + 1, win, d), BF16),
                pltpu.SemaphoreType.DMA((DSP_DEPTH + 1,)),
                pltpu.VMEM((tm, d), F32),
                pltpu.VMEM((tm, 1), F32),
                pltpu.SMEM((6,), I32),
            ],
        ),
        compiler_params=_params(("arbitrary",)),
        name="moe_dispatch",
    )(meta, wcum_flat, h, pos_w, gate_w)


def _expert_kernel(be_ref, used_ref, xb_ref, gate_ref, w1_ref, w3_ref, w2_ref, yb_ref, acc_ref):
    b = pl.program_id(0)
    j = pl.program_id(1)
    last = pl.num_programs(1) - 1
    live = b < used_ref[0]

    @pl.when((b == 0) & (j == 0))
    def _():
        acc_ref[...] = jnp.zeros(acc_ref.shape, F32)

    @pl.when(live)
    def _():
        rows = xb_ref.shape[0] // ROW_SPLIT
        for r in range(ROW_SPLIT):
            sl = slice(r * rows, (r + 1) * rows)
            xb = xb_ref[sl, :]
            t = (_silu(_dot(xb, w1_ref[...])) * _dot(xb, w3_ref[...])).astype(BF16)
            part = _dot(t, w2_ref[...])
            acc_ref[sl, :] = jnp.where(j == 0, 0.0, acc_ref[sl, :]) + part

        @pl.when(j == last)
        def _():
            yb_ref[...] = (acc_ref[...] * gate_ref[...]).astype(BF16)

    @pl.when(jnp.logical_not(live) & (j == last))
    def _():
        yb_ref[...] = jnp.zeros(yb_ref.shape, BF16)


def _experts(block_expert, used, xb, gate_buf, w1, w3, w2, n_blocks):
    d = xb.shape[1]
    dff = w1.shape[2]
    tm = MOE_TM
    tf = dff // 2 if (dff // 2) % LANES == 0 else dff
    nj = dff // tf

    def jj(b, j, be, used):
        return jnp.where(b < used[0], j, nj - 1)

    return pl.pallas_call(
        _expert_kernel,
        out_shape=jax.ShapeDtypeStruct((n_blocks * tm, d), BF16),
        grid_spec=pltpu.PrefetchScalarGridSpec(
            num_scalar_prefetch=2,
            grid=(n_blocks, nj),
            in_specs=[
                pl.BlockSpec((tm, d), lambda b, j, be, used: (b, 0)),
                pl.BlockSpec((tm, 1), lambda b, j, be, used: (b, 0)),
                pl.BlockSpec((None, d, tf), lambda b, j, be, used: (be[b], 0, jj(b, j, be, used))),
                pl.BlockSpec((None, d, tf), lambda b, j, be, used: (be[b], 0, jj(b, j, be, used))),
                pl.BlockSpec((None, tf, d), lambda b, j, be, used: (be[b], jj(b, j, be, used), 0)),
            ],
            out_specs=pl.BlockSpec((tm, d), lambda b, j, be, used: (b, 0)),
            scratch_shapes=[pltpu.VMEM((tm, d), F32)],
        ),
        compiler_params=_params(("arbitrary", "arbitrary")),
        name="moe_experts",
    )(block_expert, used, xb, gate_buf, w1, w3, w2)


def _combine_kernel(ws_ref, nsub_ref, yb_hbm, pos_ref, x_ref, mod_ref, g_ref, o_ref,
                    buf_ref, sem_ref, xbuf_ref, xsem_ref, y_ref):
    i = pl.program_id(0)
    n_steps = pl.num_programs(0)
    ne = N_EXPERTS
    t, cw = CMB_T, CMB_WIN

    def start_row(step, e, sub):
        return pl.multiple_of(ws_ref[step * ne + e] + sub * cw, BF16_ROWS)

    def copy(step, e, slot):
        return pltpu.make_async_copy(yb_hbm.at[pl.ds(start_row(step, e, 0), cw), :],
                                     buf_ref.at[slot, e], sem_ref.at[slot, e])

    @pl.when(i == 0)
    def _():
        for e in range(ne):
            copy(0, e, 0).start()

    slot = i % 2

    @pl.when(i + 1 < n_steps)
    def _():
        for e in range(ne):
            copy(i + 1, e, 1 - slot).start()

    pe = pos_ref[...]
    pos = pe[:, 0:TOP_K]
    top = pe[:, TOP_K:2 * TOP_K]
    col = lax.broadcasted_iota(I32, (t, cw), 1)

    def onehot(rel):
        return jnp.where(rel[:, 0:1] == col, 1.0, jnp.where(rel[:, 1:2] == col, 1.0, 0.0)).astype(BF16)

    rels = []
    for e in range(ne):
        copy(i, e, slot).wait()
        rels.append(jnp.where(top == e, pos - ws_ref[i * ne + e], -1))
    picks = jnp.concatenate([onehot(rel) for rel in rels], axis=1)
    y_ref[...] = _dot(picks, buf_ref[slot].reshape(ne * cw, buf_ref.shape[-1]))

    for e in range(ne):
        nsub = nsub_ref[i * ne + e]
        for sub in range(1, CMB_MAX_SUB):
            @pl.when(nsub > sub)
            def _():
                extra = pltpu.make_async_copy(yb_hbm.at[pl.ds(start_row(i, e, sub), cw), :],
                                              xbuf_ref, xsem_ref)
                extra.start()
                extra.wait()
                y_ref[...] += _dot(onehot(rels[e] - sub * cw), xbuf_ref[...])

    o_ref[...] = x_ref[...] + mod_ref[5:6, :] * _rms(y_ref[...], g_ref[...])


def _combine(win_start, nsub, yb, pos, x2, mod_l, g, rows_per_batch):
    n, d = x2.shape
    t = CMB_T
    bpb = rows_per_batch // t
    return pl.pallas_call(
        _combine_kernel,
        out_shape=jax.ShapeDtypeStruct((n, d), F32),
        grid_spec=pltpu.PrefetchScalarGridSpec(
            num_scalar_prefetch=2,
            grid=(n // t,),
            in_specs=[
                pl.BlockSpec(memory_space=pl.ANY),
                pl.BlockSpec((t, 2 * TOP_K), lambda i, ws, ns: (i, 0)),
                pl.BlockSpec((t, d), lambda i, ws, ns: (i, 0)),
                pl.BlockSpec((None, 6, d), lambda i, ws, ns: (i // bpb, 0, 0)),
                pl.BlockSpec((1, d), lambda i, ws, ns: (0, 0)),
            ],
            out_specs=pl.BlockSpec((t, d), lambda i, ws, ns: (i, 0)),
            scratch_shapes=[
                pltpu.VMEM((2, N_EXPERTS, CMB_WIN, d), BF16),
                pltpu.SemaphoreType.DMA((2, N_EXPERTS)),
                pltpu.VMEM((CMB_WIN, d), BF16),
                pltpu.SemaphoreType.DMA(()),
                pltpu.VMEM((t, d), F32),
            ],
        ),
        compiler_params=_params(("arbitrary",)),
        name="moe_combine",
    )(win_start, nsub, yb, pos, x2, mod_l, g)


def _moe(x2, attn, sc, w_out, mod_l, gmix, gpre, gpost, w_router, w1, w3, w2, rows_per_batch):
    n, d = x2.shape
    ne, tm, dtm, win, t = N_EXPERTS, MOE_TM, DSP_TM, DSP_WIN, CMB_T
    wr_hi, wr_lo = _split2(w_router)
    wr_packed = jnp.zeros((d, LANES), BF16).at[:, 0:ne].set(wr_hi).at[:, ne:2 * ne].set(wr_lo)
    x2, h, route, wcum = _router(x2, attn, sc, w_out, mod_l, gmix, gpre, wr_packed, rows_per_batch)

    top = route[:, 0:TOP_K].astype(I32)
    gates = route[:, TOP_K:2 * TOP_K]
    rank = route[:, 2 * TOP_K:3 * TOP_K].astype(I32)
    wc = wcum.reshape(n // win, LANES)[:, 0:ne].astype(I32)
    counts = wc[-1]
    padded = (counts + tm - 1) // tm * tm
    pend = jnp.cumsum(padded)
    pstart = pend - padded
    experts = jnp.arange(ne, dtype=I32)
    pos = rank + jnp.sum(jnp.where(top[:, :, None] == experts, pstart, 0), axis=-1)

    n_blocks = (n * TOP_K) // tm + ne + 1
    blk = jnp.arange(n_blocks, dtype=I32) * tm
    block_expert = jnp.minimum(jnp.sum((pend[None, :] <= blk[:, None]).astype(I32), axis=1), ne - 1)
    used = (pend[-1] // tm).astype(I32).reshape(1)

    n_steps = n_blocks * (tm // dtm)
    meta = jnp.concatenate([pstart, counts, pend]).astype(I32)
    pos_w = pos.T.reshape(TOP_K, n // win, win).transpose(1, 0, 2)
    gate_w = gates.T.reshape(TOP_K, n // win, win).transpose(1, 0, 2)
    xb, gate_buf = _dispatch(meta, wc.T.reshape(-1), h, pos_w, gate_w, n_steps)
    yb = _experts(block_expert, used, xb, gate_buf, w1, w3, w2, n_blocks)

    at_block_end = wc[t // win - 1::t // win]
    before_block = jnp.concatenate([jnp.zeros((1, ne), I32), at_block_end[:-1]], axis=0)
    base_b = pstart[None, :] + before_block
    end_b = pstart[None, :] + at_block_end
    win_start = base_b // BF16_ROWS * BF16_ROWS
    nsub = jnp.where(end_b > base_b, (end_b - win_start + CMB_WIN - 1) // CMB_WIN, 0)
    return _combine(win_start.astype(I32).reshape(-1), nsub.astype(I32).reshape(-1), yb,
                    jnp.concatenate([pos, top], axis=1), x2, mod_l, gpost, rows_per_batch)


def _reorder_w_in(w):
    main = jnp.concatenate([w[:, 0:3072], w[:, 3080:4616]], axis=1).astype(BF16)
    wdt = jnp.zeros((w.shape[0], LANES), BF16).at[:, 0:SSD_HEADS].set(w[:, 3072:3080].astype(BF16))
    return main, wdt


def _pad_row(v, width=LANES):
    return jnp.zeros((1, width), F32).at[0, 0:v.shape[0]].set(v)


def kernel(x, c, w_mod, b_mod, norm_mix_pre, norm_mix_post, norm_ffn_pre, norm_ffn_post, w_in, w_out, lambda_qk, attn_subln, ssd_conv_w, ssd_conv_b, ssd_dt_bias, ssd_a_log, ssd_d, ssd_norm, sconv_w, ffn_w1, ffn_w3, ffn_w2, moe_router, moe_w1, moe_w3, moe_w2):
    batch, seq, d = x.shape
    depth = w_mod.shape[0]
    n = batch * seq
    assert seq % SSD_CHUNK == 0 and w_in.shape[2] == PROJ_COLS + SSD_HEADS
    mod = _modulation(c, w_mod, b_mod).reshape(depth, batch, 6, d)
    x2 = x.reshape(n, d)
    for i in range(depth):
        mod_l = mod[i]
        w_main, w_dt = _reorder_w_in(w_in[i])
        proj, dt_raw = _inproj(x2, mod_l, norm_mix_pre[i][None, :], w_main, w_dt, seq)
        lam_init = 0.8 - 0.6 * math.exp(-0.3 * i)
        attn = _attention(proj, lambda_qk[i], attn_subln[i][None, :], batch, seq, lam_init)
        sc = _ssd_sconv(proj, dt_raw, ssd_conv_w[i], ssd_conv_b[i][None, :], _pad_row(ssd_dt_bias[i]),
                        _pad_row(ssd_a_log[i]), jnp.repeat(ssd_d[i], SSD_HEAD_DIM)[None, :],
                        ssd_norm[i][None, :], sconv_w[i], batch, seq)
        mix = (x2, attn, sc, w_out[i].astype(BF16), mod_l, norm_mix_post[i][None, :])
        if i % 2 == 0:
            x2 = _ffn(*mix, norm_ffn_pre[i][None, :], norm_ffn_post[i][None, :],
                      ffn_w1[i // 2].astype(BF16), ffn_w3[i // 2].astype(BF16), ffn_w2[i // 2].astype(BF16), seq)
        else:
            x2 = _moe(*mix, norm_ffn_pre[i][None, :], norm_ffn_post[i][None, :], moe_router[i // 2],
                      moe_w1[i // 2].astype(BF16), moe_w3[i // 2].astype(BF16), moe_w2[i // 2].astype(BF16), seq)
    return x2.reshape(batch, seq, d)
```
